```python
import jax
import jax.numpy as jnp
from jax import lax
import numpy as np

D_MODEL = 1024
BATCH = 32
SEQ = 2048
DEPTH = 2

N_MIXERS = 2
MIX_WIDTH = D_MODEL
X_WIDTH = D_MODEL // 4
N_X_HEADS = 4
X_HEAD_DIM = X_WIDTH // N_X_HEADS
SEQ_MIX_WIDTH = MIX_WIDTH - X_WIDTH
LIN_HEAD_DIM = 128
N_LIN_HEADS = SEQ_MIX_WIDTH // LIN_HEAD_DIM
CONV_WIDTH = 4
CHUNK = 64
SB_HEAD_DIM = 64
N_SB_HEADS = SEQ_MIX_WIDTH // SB_HEAD_DIM
SB_BLOCK = 128
N_MEM = 256
D_FF = 4 * D_MODEL
EPS = 1e-6
N_A_LAYERS = (DEPTH + 1) // 2
N_B_LAYERS = DEPTH // 2
IN_A = 4 * SEQ_MIX_WIDTH + 2 * N_LIN_HEADS + X_WIDTH
IN_B = 3 * SEQ_MIX_WIDTH + X_WIDTH

kernel_name = 'hybrid_gdn_stickbreak_memory_trunk'


def rms_norm(x, g):
    xf = x.astype(jnp.float32)
    y = xf * lax.rsqrt(jnp.mean(xf * xf, axis=-1, keepdims=True) + EPS)
    return (y * g.astype(jnp.float32)).astype(x.dtype)


def l2norm(x):
    return x * lax.rsqrt(jnp.sum(x * x, axis=-1, keepdims=True) + EPS)


def causal_conv(x, w):
    c = x.shape[-1]
    return lax.conv_general_dilated(
        x, w[:, None, :].astype(x.dtype), window_strides=(1,),
        padding=[(CONV_WIDTH - 1, 0)], dimension_numbers=('NWC', 'WIO', 'NWC'),
        feature_group_count=c)


def gated_deltanet(p, conv_w, a_log, dt_bias, o_gain):
    B, S, _ = p.shape
    H, Dh, C, W = N_LIN_HEADS, LIN_HEAD_DIM, CHUNK, SEQ_MIX_WIDTH
    nc = S // C
    qkv = jax.nn.silu(causal_conv(p[..., :3 * W], conv_w)).astype(jnp.float32)
    gate = p[..., 3 * W:4 * W].astype(jnp.float32)
    beta = jax.nn.sigmoid(p[..., 4 * W:4 * W + H].astype(jnp.float32))
    g = -jnp.exp(a_log.astype(jnp.float32)) * jax.nn.softplus(
        p[..., 4 * W + H:].astype(jnp.float32) + dt_bias.astype(jnp.float32))
    q = l2norm(qkv[..., :W].reshape(B, S, H, Dh)) * (Dh ** -0.5)
    k = l2norm(qkv[..., W:2 * W].reshape(B, S, H, Dh))
    v = qkv[..., 2 * W:].reshape(B, S, H, Dh)

    def chunked(t):
        t = t.reshape((B, nc, C) + t.shape[2:])
        return jnp.moveaxis(jnp.moveaxis(t, 1, 0), 3, 2)

    q, k, v, beta = chunked(q), chunked(k), chunked(v), chunked(beta)
    gc = jnp.cumsum(chunked(g), axis=-1)
    idx = jnp.arange(C)
    causal = idx[:, None] >= idx[None, :]
    strict = idx[:, None] > idx[None, :]
    decay = jnp.exp(jnp.where(causal, gc[..., :, None] - gc[..., None, :], -jnp.inf))
    kb = k * beta[..., None]
    lower = jnp.where(strict, jnp.einsum('nbhcd,nbhsd->nbhcs', kb, k) * decay, 0.0)
    eye = jnp.eye(C, dtype=jnp.float32)
    rhs = jnp.concatenate([v * beta[..., None], kb * jnp.exp(gc)[..., None]], axis=-1)
    sol = lax.linalg.triangular_solve(eye + lower, rhs, left_side=True, lower=True,
                                      unit_diagonal=True)
    u, w = sol[..., :Dh], sol[..., Dh:]
    intra = jnp.einsum('nbhcd,nbhsd->nbhcs', q, k) * decay
    q_dec = q * jnp.exp(gc)[..., None]
    k_dec = k * jnp.exp(gc[..., -1:] - gc)[..., None]
    chunk_decay = jnp.exp(gc[..., -1])

    def step(state, inp):
        u_c, w_c, q_c, k_c, a_c, d_c = inp
        v_new = u_c - jnp.einsum('bhcd,bhde->bhce', w_c, state)
        o_c = jnp.einsum('bhcd,bhde->bhce', q_c, state) + jnp.einsum('bhcs,bhse->bhce', a_c, v_new)
        state = state * d_c[..., None, None] + jnp.einsum('bhcd,bhce->bhde', k_c, v_new)
        return state, o_c

    state0 = jnp.zeros((B, H, Dh, Dh), jnp.float32)
    _, o = lax.scan(step, state0, (u, w, q_dec, k_dec, intra, chunk_decay))
    o = jnp.swapaxes(jnp.moveaxis(o, 0, 1), 2, 3).reshape(B, S, H, Dh)
    o = o * lax.rsqrt(jnp.mean(o * o, axis=-1, keepdims=True) + EPS) * o_gain.astype(jnp.float32)
    o = o * jax.nn.silu(gate.reshape(B, S, H, Dh))
    return o.reshape(B, S, W)


def stick_breaking_attention(q, k, v):
    B, S, H, Dh = q.shape
    scale = Dh ** -0.5
    outs = []
    for blk in range(S // SB_BLOCK):
        t0 = blk * SB_BLOCK
        t1 = t0 + SB_BLOCK
        kb, vb = k[:, :t1], v[:, :t1]
        z = jnp.einsum('bthd,bshd->bhts', q[:, t0:t1], kb,
                       preferred_element_type=jnp.float32) * scale
        t_idx = t0 + jnp.arange(SB_BLOCK)[:, None]
        s_idx = jnp.arange(t1)[None, :]
        before = s_idx < t_idx
        log_beta = jax.nn.log_sigmoid(z)
        log_1m_beta = jnp.where(before, jax.nn.log_sigmoid(-z), 0.0)
        tail = lax.cumsum(log_1m_beta, axis=3, reverse=True) - log_1m_beta
        a = jnp.where(before, jnp.exp(log_beta + tail), 0.0)
        outs.append(jnp.einsum('bhts,bshd->bthd', a.astype(vb.dtype), vb))
    return jnp.concatenate(outs, axis=1).reshape(B, S, H * Dh)


def memory_attention(q, mem_kv):
    B, S, _ = q.shape
    q = q.reshape(B, S, N_X_HEADS, X_HEAD_DIM)
    k = mem_kv[..., :X_WIDTH].reshape(B, N_MEM, N_X_HEADS, X_HEAD_DIM)
    v = mem_kv[..., X_WIDTH:].reshape(B, N_MEM, N_X_HEADS, X_HEAD_DIM)
    s = jnp.einsum('bshd,bmhd->bhsm', q, k, preferred_element_type=jnp.float32) * (X_HEAD_DIM ** -0.5)
    p = jax.nn.softmax(s, axis=-1).astype(v.dtype)
    return jnp.einsum('bhsm,bmhd->bshd', p, v).reshape(B, S, X_WIDTH)


def setup_inputs(seed: int = 0) -> dict:
    key = jax.random.key(seed)
    ks = jax.random.split(key, 18)
    f32 = jnp.float32
    nrm = lambda k, shape, scale: jax.random.normal(k, shape, f32) * scale
    gain = lambda k, shape: 1.0 + 0.1 * jax.random.normal(k, shape, f32)
    return {
        'x': nrm(ks[0], (BATCH, SEQ, D_MODEL), 1.0),
        'mem': nrm(ks[1], (BATCH, N_MEM, D_MODEL), 1.0),
        'mem_norm': gain(ks[2], (D_MODEL,)),
        'norm_pre_mix': gain(ks[3], (DEPTH, D_MODEL)),
        'norm_post_mix': gain(ks[4], (DEPTH, D_MODEL)),
        'norm_pre_mlp': gain(ks[5], (DEPTH, D_MODEL)),
        'norm_post_mlp': gain(ks[6], (DEPTH, D_MODEL)),
        'w_in_a': nrm(ks[7], (N_A_LAYERS, D_MODEL, IN_A), D_MODEL ** -0.5),
        'conv_w_a': nrm(ks[8], (N_A_LAYERS, CONV_WIDTH, 3 * SEQ_MIX_WIDTH), CONV_WIDTH ** -0.5),
        'a_log_a': jnp.log(jax.random.uniform(ks[9], (N_A_LAYERS, N_LIN_HEADS), f32, 0.01, 1.0)),
        'dt_bias_a': nrm(ks[10], (N_A_LAYERS, N_LIN_HEADS), 0.1),
        'onorm_a': gain(ks[11], (N_A_LAYERS, LIN_HEAD_DIM)),
        'w_in_b': nrm(ks[12], (N_B_LAYERS, D_MODEL, IN_B), D_MODEL ** -0.5),
        'w_mem_kv': nrm(ks[13], (DEPTH, D_MODEL, 2 * X_WIDTH), D_MODEL ** -0.5),
        'w_out': nrm(ks[14], (DEPTH, MIX_WIDTH, D_MODEL), MIX_WIDTH ** -0.5),
        'w_up': nrm(ks[15], (DEPTH, D_MODEL, D_FF), D_MODEL ** -0.5),
        'w_down': nrm(ks[16], (DEPTH, D_FF, D_MODEL), D_FF ** -0.5),
    }


def reference(x, mem, mem_norm, norm_pre_mix, norm_post_mix, norm_pre_mlp, norm_post_mlp,
              w_in_a, conv_w_a, a_log_a, dt_bias_a, onorm_a, w_in_b, w_mem_kv, w_out,
              w_up, w_down):
    B, S, _ = x.shape
    mem_n = rms_norm(mem, mem_norm)
    for i in range(DEPTH):
        j = i // N_MIXERS
        h = rms_norm(x, norm_pre_mix[i])
        if i % N_MIXERS == 0:
            proj = h @ w_in_a[j]
            mix = gated_deltanet(proj[..., :IN_A - X_WIDTH], conv_w_a[j], a_log_a[j],
                                 dt_bias_a[j], onorm_a[j])
            mem_q = proj[..., IN_A - X_WIDTH:]
        else:
            proj = h @ w_in_b[j]
            W = SEQ_MIX_WIDTH
            q = proj[..., :W].reshape(B, S, N_SB_HEADS, SB_HEAD_DIM)
            k = proj[..., W:2 * W].reshape(B, S, N_SB_HEADS, SB_HEAD_DIM)
            v = proj[..., 2 * W:3 * W].reshape(B, S, N_SB_HEADS, SB_HEAD_DIM)
            mix = stick_breaking_attention(q, k, v)
            mem_q = proj[..., 3 * W:]
        cross = memory_attention(mem_q, mem_n @ w_mem_kv[i])
        y = jnp.concatenate([mix.astype(x.dtype), cross.astype(x.dtype)], axis=-1) @ w_out[i]
        x = x + rms_norm(y, norm_post_mix[i])
        h = rms_norm(x, norm_pre_mlp[i])
        y = jnp.square(jax.nn.relu(h @ w_up[i])) @ w_down[i]
        x = x + rms_norm(y, norm_post_mlp[i])
    return x
```

```python
import functools

import jax
import jax.numpy as jnp
from jax import lax
from jax.experimental import pallas as pl
from jax.experimental.pallas import tpu as pltpu

EPS = 1e-6
LANES = 128
CONV_WIDTH = 4
GDN_HEAD_DIM = 128
GDN_CHUNK = 64
SB_HEAD_DIM = 64
X_HEADS = 4
X_HEAD_DIM = 64
N_MEM = 256
VMEM_LIMIT = 56 * 1024 * 1024

MXU_DTYPE = jnp.bfloat16
F32 = jnp.float32
HIGHEST = lax.Precision.HIGHEST


def _dot(a, b):
    return jnp.dot(a.astype(MXU_DTYPE), b.astype(MXU_DTYPE), preferred_element_type=F32)


def _dot_nt(a, b):
    return lax.dot_general(a.astype(MXU_DTYPE), b.astype(MXU_DTYPE),
                           (((1,), (1,)), ((), ())), preferred_element_type=F32)


def _dot_tn(a, b):
    return lax.dot_general(a.astype(MXU_DTYPE), b.astype(MXU_DTYPE),
                           (((0,), (0,)), ((), ())), preferred_element_type=F32)


def _dot_f32(a, b):
    return jnp.dot(a, b, precision=HIGHEST, preferred_element_type=F32)


def _rms(x, g):
    return x * lax.rsqrt(jnp.mean(x * x, axis=-1, keepdims=True) + EPS) * g


def _sigmoid(x):
    return 1.0 / (1.0 + jnp.exp(-x))


def _softplus(x):
    return jnp.maximum(x, 0.0) + jnp.log1p(jnp.exp(-jnp.abs(x)))


def _col_tile(n, cap=512):
    t = cap
    while n % t:
        t -= LANES
    return t


def _norm_matmul_kernel(x_ref, g_ref, w_ref, o_ref, *, tn):
    h = _rms(x_ref[...], g_ref[...]).astype(MXU_DTYPE)
    for c in range(0, w_ref.shape[1], tn):
        o_ref[:, c:c + tn] = jnp.dot(h, w_ref[:, c:c + tn], preferred_element_type=F32)


def _norm_matmul(x, g, w, tm):
    t, d = x.shape
    n = w.shape[1]
    return pl.pallas_call(
        functools.partial(_norm_matmul_kernel, tn=_col_tile(n)),
        grid=(t // tm,),
        in_specs=[pl.BlockSpec((tm, d), lambda i: (i, 0)),
                  pl.BlockSpec((1, d), lambda i: (0, 0)),
                  pl.BlockSpec((d, n), lambda i: (0, 0))],
        out_specs=pl.BlockSpec((tm, n), lambda i: (i, 0)),
        out_shape=jax.ShapeDtypeStruct((t, n), F32),
        compiler_params=pltpu.CompilerParams(dimension_semantics=("parallel",),
                                             vmem_limit_bytes=VMEM_LIMIT),
        name="norm_matmul",
    )(x, g.reshape(1, d), w)


def _gdn_kernel(q_ref, k_ref, v_ref, gate_ref, ba_ref, cwq_ref, cwk_ref, cwv_ref,
                alog_ref, dtb_ref, ogain_ref, o_ref,
                qn_s, kn_s, vv_s, beta_s, g_s, u_s, w_s, qd_s, kd_s, a_s, cd_s, oo_s,
                *, n_heads):
    head = pl.program_id(1)
    seq = q_ref.shape[0]
    dh = GDN_HEAD_DIM
    ch = GDN_CHUNK
    nc = seq // ch

    row = lax.broadcasted_iota(jnp.int32, (seq, dh), 0)

    def conv_silu(x_ref, cw_ref):
        x = x_ref[...]
        cw = cw_ref[...]
        y = x * cw[CONV_WIDTH - 1:CONV_WIDTH, :]
        for sh in range(1, CONV_WIDTH):
            xs = jnp.where(row >= sh, pltpu.roll(x, sh, 0), 0.0)
            y = y + xs * cw[CONV_WIDTH - 1 - sh:CONV_WIDTH - sh, :]
        return y * _sigmoid(y)

    def l2n(x):
        return x * lax.rsqrt(jnp.sum(x * x, axis=-1, keepdims=True) + EPS)

    qn_s[...] = l2n(conv_silu(q_ref, cwq_ref)) * (dh ** -0.5)
    kn_s[...] = l2n(conv_silu(k_ref, cwk_ref))
    vv_s[...] = conv_silu(v_ref, cwv_ref)

    ba = ba_ref[...]
    lane = lax.broadcasted_iota(jnp.int32, (seq, LANES), 1)
    beta_all = _sigmoid(ba)
    g_all = -jnp.exp(alog_ref[...]) * _softplus(ba + dtb_ref[...])
    beta_col = jnp.sum(jnp.where(lane == head, beta_all, 0.0), axis=-1, keepdims=True)
    g_col = jnp.sum(jnp.where(lane == head + n_heads, g_all, 0.0), axis=-1, keepdims=True)
    beta_s[...] = jnp.broadcast_to(beta_col, (seq, dh))
    g_s[...] = jnp.broadcast_to(g_col, (seq, dh))

    ri = lax.broadcasted_iota(jnp.int32, (ch, ch), 0)
    ci = lax.broadcasted_iota(jnp.int32, (ch, ch), 1)
    causal = ri >= ci
    strict = ri > ci
    tri_incl = jnp.where(causal, 1.0, 0.0).astype(F32)
    upper_incl = ri <= ci
    ones_cc = jnp.ones((ch, ch), F32)
    eye = jnp.where(ri == ci, 1.0, 0.0).astype(F32)

    def chunk_prep(c, _):
        sl = pl.ds(pl.multiple_of(c * ch, ch), ch)
        q = qn_s[sl, :]
        k = kn_s[sl, :]
        v = vv_s[sl, :]
        bb = beta_s[sl, :]
        gb = g_s[sl, :]
        gc = _dot_f32(tri_incl, gb)
        gc_row = _dot_f32(ones_cc, jnp.where(upper_incl, gb[:, :ch], 0.0))
        decay = jnp.exp(jnp.where(causal, gc[:, :ch] - gc_row, -1e30))
        egc = jnp.exp(gc)
        kb = k * bb
        kq = _dot_nt(jnp.concatenate([kb, q], axis=0), k)
        low = jnp.where(strict, kq[:ch] * decay, 0.0)
        intra = kq[ch:] * decay
        inv = eye - low
        pw = low
        n = 2
        while n < ch + 1:
            pw = _dot_f32(pw, pw)
            inv = inv + _dot_f32(inv, pw)
            n *= 2
        rhs = jnp.concatenate([v * bb, kb * egc], axis=-1)
        sol = _dot_f32(inv, rhs)
        u_s[sl, :] = sol[:, :dh]
        w_s[sl, :] = sol[:, dh:]
        qd_s[sl, :] = q * egc
        gl = gc[ch - 1:ch, :]
        kd_s[sl, :] = k * jnp.exp(gl - gc)
        a_s[sl, :] = intra
        cd_s[c] = jnp.exp(gl)
        return 0

    lax.fori_loop(0, nc, chunk_prep, 0)

    def chunk_scan(c, state):
        sl = pl.ds(pl.multiple_of(c * ch, ch), ch)
        wq = jnp.concatenate([w_s[sl, :], qd_s[sl, :]], axis=0)
        r = _dot(wq, state)
        v_new = u_s[sl, :] - r[:ch]
        oo_s[sl, :] = r[ch:] + _dot(a_s[sl, :], v_new)
        return state * cd_s[c] + _dot_tn(kd_s[sl, :], v_new)

    lax.fori_loop(0, nc, chunk_scan, jnp.zeros((dh, dh), F32))

    o = oo_s[...]
    gate = gate_ref[...]
    o = o * lax.rsqrt(jnp.mean(o * o, axis=-1, keepdims=True) + EPS) * ogain_ref[...]
    o_ref[...] = o * (gate * _sigmoid(gate))


def _gdn(proj, conv_w, a_log, dt_bias, o_gain, batch, seq, n_heads):
    dh = GDN_HEAD_DIM
    width = n_heads * dh
    ba_blk = proj.shape[1] // LANES - 1
    pad = LANES - 2 * n_heads
    alog_row = jnp.concatenate([jnp.zeros((n_heads,), F32), a_log, jnp.zeros((pad,), F32)]).reshape(1, LANES)
    dtb_row = jnp.concatenate([jnp.zeros((n_heads,), F32), dt_bias, jnp.zeros((pad,), F32)]).reshape(1, LANES)
    col = lambda off: pl.BlockSpec((seq, dh), lambda b, h: (b, h + off))
    cw = lambda off: pl.BlockSpec((CONV_WIDTH, dh), lambda b, h: (0, h + off))
    row = pl.BlockSpec((1, LANES), lambda b, h: (0, 0))
    big = pltpu.VMEM((seq, dh), F32)
    return pl.pallas_call(
        functools.partial(_gdn_kernel, n_heads=n_heads),
        grid=(batch, n_heads),
        in_specs=[col(0), col(n_heads), col(2 * n_heads), col(3 * n_heads),
                  pl.BlockSpec((seq, LANES), lambda b, h: (b, ba_blk)),
                  cw(0), cw(n_heads), cw(2 * n_heads), row, row, row],
        out_specs=pl.BlockSpec((seq, dh), lambda b, h: (b, h)),
        out_shape=jax.ShapeDtypeStruct((batch * seq, width), F32),
        scratch_shapes=[big, big, big, big, big, big, big, big, big,
                        pltpu.VMEM((seq, GDN_CHUNK), F32),
                        pltpu.VMEM((seq // GDN_CHUNK, 1, dh), F32),
                        big],
        compiler_params=pltpu.CompilerParams(dimension_semantics=("parallel", "parallel"),
                                             vmem_limit_bytes=VMEM_LIMIT),
        name="gated_deltanet",
    )(proj, proj, proj, proj, proj, conv_w, conv_w, conv_w, alog_row, dtb_row,
      o_gain.reshape(1, dh))


def _sb_kernel(q_ref, k_ref, v_ref, o_ref, *, blk):
    qi = pl.program_id(2)
    dh = SB_HEAD_DIM
    ri = lax.broadcasted_iota(jnp.int32, (blk, blk), 0)
    ci = lax.broadcasted_iota(jnp.int32, (blk, blk), 1)
    before = ci < ri
    after_ones = jnp.where(ri > ci, 1.0, 0.0).astype(MXU_DTYPE)

    def block(q, k, v, carry, masked):
        z = _dot_nt(q, k)
        lb = jnp.minimum(z, 0.0) - jnp.log1p(jnp.exp(-jnp.abs(z)))
        l1m = lb - z
        if masked:
            l1m = jnp.where(before, l1m, 0.0)
        hi = l1m.astype(MXU_DTYPE)
        lo = l1m - hi.astype(F32)
        tail = (jnp.dot(hi, after_ones, preferred_element_type=F32)
                + jnp.dot(lo.astype(MXU_DTYPE), after_ones, preferred_element_type=F32))
        a = jnp.exp(lb + tail + carry)
        if masked:
            a = jnp.where(before, a, 0.0)
        return _dot(a, v), carry + jnp.sum(l1m, axis=-1, keepdims=True)

    outs = []
    for hh in range(LANES // dh):
        cols = slice(hh * dh, (hh + 1) * dh)
        q = q_ref[:, cols] * (dh ** -0.5)
        d0 = pl.ds(pl.multiple_of(qi * blk, blk), blk)
        acc, carry = block(q, k_ref[d0, cols], v_ref[d0, cols], jnp.zeros((blk, 1), F32), True)

        def body(i, st):
            acc, carry = st
            ks = pl.ds(pl.multiple_of((qi - 1 - i) * blk, blk), blk)
            o, carry = block(q, k_ref[ks, cols], v_ref[ks, cols], carry, False)
            return acc + o, carry

        acc, _ = lax.fori_loop(0, qi, body, (acc, carry))
        outs.append(acc)
    o_ref[...] = jnp.concatenate(outs, axis=-1)


def _stick_breaking(proj, batch, seq, width, blk=256):
    npair = width // LANES
    nq = seq // blk
    return pl.pallas_call(
        functools.partial(_sb_kernel, blk=blk),
        grid=(batch, npair, nq),
        in_specs=[pl.BlockSpec((blk, LANES), lambda b, p, i: (b * nq + i, p)),
                  pl.BlockSpec((seq, LANES), lambda b, p, i: (b, npair + p)),
                  pl.BlockSpec((seq, LANES), lambda b, p, i: (b, 2 * npair + p))],
        out_specs=pl.BlockSpec((blk, LANES), lambda b, p, i: (b * nq + i, p)),
        out_shape=jax.ShapeDtypeStruct((batch * seq, width), F32),
        compiler_params=pltpu.CompilerParams(
            dimension_semantics=("parallel", "parallel", "arbitrary"),
            vmem_limit_bytes=VMEM_LIMIT),
        name="stick_breaking",
    )(proj, proj, proj)


def _post_mix_kernel(x_ref, mix_ref, mq_ref, mk_ref, mv_ref, wo_ref, g_ref, o_ref):
    dh = X_HEAD_DIM
    mq = mq_ref[...]
    mk = mk_ref[...]
    mv = mv_ref[...]
    heads = []
    for h in range(X_HEADS):
        cols = slice(h * dh, (h + 1) * dh)
        s = _dot_nt(mq[:, cols] * (dh ** -0.5), mk[:, cols])
        p = jnp.exp(s - jnp.max(s, axis=-1, keepdims=True))
        inv_l = 1.0 / jnp.sum(p, axis=-1, keepdims=True)
        heads.append(_dot(p * inv_l, mv[:, cols]))
    cross = jnp.concatenate(heads, axis=-1)
    wmix = mix_ref.shape[1]
    y = _dot(mix_ref[...], wo_ref[:wmix, :]) + _dot(cross, wo_ref[wmix:, :])
    o_ref[...] = x_ref[...] + _rms(y, g_ref[...])


def _post_mix(x, mix, proj, memq_blk, mem_kv, layer, wo, g, batch, seq, tm):
    t, d = x.shape
    wmix = mix.shape[1]
    xw = X_HEADS * X_HEAD_DIM
    nt = seq // tm
    return pl.pallas_call(
        _post_mix_kernel,
        grid=(batch, nt),
        in_specs=[pl.BlockSpec((tm, d), lambda b, i: (b * nt + i, 0)),
                  pl.BlockSpec((tm, wmix), lambda b, i: (b * nt + i, 0)),
                  pl.BlockSpec((tm, xw), lambda b, i: (b * nt + i, memq_blk)),
                  pl.BlockSpec((N_MEM, xw), lambda b, i: (b, 2 * layer)),
                  pl.BlockSpec((N_MEM, xw), lambda b, i: (b, 2 * layer + 1)),
                  pl.BlockSpec((wmix + xw, d), lambda b, i: (0, 0)),
                  pl.BlockSpec((1, d), lambda b, i: (0, 0))],
        out_specs=pl.BlockSpec((tm, d), lambda b, i: (b * nt + i, 0)),
        out_shape=jax.ShapeDtypeStruct((t, d), F32),
        compiler_params=pltpu.CompilerParams(dimension_semantics=("parallel", "parallel"),
                                             vmem_limit_bytes=VMEM_LIMIT),
        name="post_mix",
    )(x, mix, proj, mem_kv, mem_kv, wo, g.reshape(1, d))


def _mlp_kernel(x_ref, g1_ref, wu_ref, wd_ref, g2_ref, o_ref, *, tf):
    x = x_ref[...]
    h = _rms(x, g1_ref[...]).astype(MXU_DTYPE)
    acc = jnp.zeros(x.shape, F32)
    for c in range(0, wu_ref.shape[1], tf):
        u = jnp.dot(h, wu_ref[:, c:c + tf], preferred_element_type=F32)
        u = jnp.square(jnp.maximum(u, 0.0)).astype(MXU_DTYPE)
        acc = acc + jnp.dot(u, wd_ref[c:c + tf, :], preferred_element_type=F32)
    o_ref[...] = x + _rms(acc, g2_ref[...])


def _mlp(x, g1, wu, wd, g2, tm, tf=1024):
    t, d = x.shape
    f = wu.shape[1]
    return pl.pallas_call(
        functools.partial(_mlp_kernel, tf=tf),
        grid=(t // tm,),
        in_specs=[pl.BlockSpec((tm, d), lambda i: (i, 0)),
                  pl.BlockSpec((1, d), lambda i: (0, 0)),
                  pl.BlockSpec((d, f), lambda i: (0, 0)),
                  pl.BlockSpec((f, d), lambda i: (0, 0)),
                  pl.BlockSpec((1, d), lambda i: (0, 0))],
        out_specs=pl.BlockSpec((tm, d), lambda i: (i, 0)),
        out_shape=jax.ShapeDtypeStruct((t, d), F32),
        compiler_params=pltpu.CompilerParams(dimension_semantics=("parallel",),
                                             vmem_limit_bytes=VMEM_LIMIT),
        name="mlp",
    )(x, g1.reshape(1, d), wu, wd, g2.reshape(1, d))


def kernel(x, mem, mem_norm, norm_pre_mix, norm_post_mix, norm_pre_mlp, norm_post_mlp,
           w_in_a, conv_w_a, a_log_a, dt_bias_a, onorm_a, w_in_b, w_mem_kv, w_out,
           w_up, w_down):
    batch, seq, d = x.shape
    depth = norm_pre_mix.shape[0]
    n_lin = a_log_a.shape[1]
    wseq = n_lin * GDN_HEAD_DIM
    xw = X_HEADS * X_HEAD_DIM
    tm = min(512, seq)
    wt = MXU_DTYPE

    xf = x.reshape(batch * seq, d)
    w_kv = jnp.concatenate([w_mem_kv[i] for i in range(depth)], axis=1).astype(wt)
    mem_kv = _norm_matmul(mem.reshape(batch * N_MEM, d), mem_norm, w_kv, tm=min(512, batch * N_MEM))

    for i in range(depth):
        j = i // 2
        if i % 2 == 0:
            w = w_in_a[j]
            n_main = 4 * wseq
            ba_pad = jnp.zeros((d, LANES - 2 * n_lin), w.dtype)
            w = jnp.concatenate([w[:, :n_main], w[:, n_main + 2 * n_lin:],
                                 w[:, n_main:n_main + 2 * n_lin], ba_pad], axis=1).astype(wt)
            proj = _norm_matmul(xf, norm_pre_mix[i], w, tm=tm)
            mix = _gdn(proj, conv_w_a[j], a_log_a[j], dt_bias_a[j], onorm_a[j], batch, seq, n_lin)
            memq_blk = n_main // xw
        else:
            proj = _norm_matmul(xf, norm_pre_mix[i], w_in_b[j].astype(wt), tm=tm)
            mix = _stick_breaking(proj, batch, seq, wseq, blk=min(256, seq))
            memq_blk = 3 * wseq // xw
        xf = _post_mix(xf, mix, proj, memq_blk, mem_kv, i, w_out[i].astype(wt), norm_post_mix[i],
                       batch, seq, tm)
        xf = _mlp(xf, norm_pre_mlp[i], w_up[i].astype(wt), w_down[i].astype(wt), norm_post_mlp[i], tm)
    return xf.reshape(batch, seq, d)
```

```python
import functools

import jax
import jax.numpy as jnp
from jax import lax
from jax.experimental import pallas as pl
from jax.experimental.pallas import tpu as pltpu

EPS = 1e-6
LANES = 128
CONV_WIDTH = 4
GDN_HEAD_DIM = 128
GDN_CHUNK = 64
SB_HEAD_DIM = 64
X_HEADS = 4
X_HEAD_DIM = 64
N_MEM = 256
VMEM_LIMIT = 56 * 1024 * 1024

MXU_DTYPE = jnp.bfloat16
F32 = jnp.float32
LOG2E = 1.4426950408889634


def _dot(a, b):
    return jnp.dot(a.astype(MXU_DTYPE), b.astype(MXU_DTYPE), preferred_element_type=F32)


def _dot_nt(a, b):
    return lax.dot_general(a.astype(MXU_DTYPE), b.astype(MXU_DTYPE),
                           (((1,), (1,)), ((), ())), preferred_element_type=F32)


def _dot_tn(a, b):
    return lax.dot_general(a.astype(MXU_DTYPE), b.astype(MXU_DTYPE),
                           (((0,), (0,)), ((), ())), preferred_element_type=F32)


def _rms(x, g):
    return x * lax.rsqrt(jnp.mean(x * x, axis=-1, keepdims=True) + EPS) * g


def _sigmoid(x):
    return 1.0 / (1.0 + jnp.exp(-x))


def _softplus(x):
    return jnp.maximum(x, 0.0) + jnp.log1p(jnp.exp(-jnp.abs(x)))


def _col_tile(n, cap=512):
    t = cap
    while n % t:
        t -= LANES
    return t


def _norm_matmul_kernel(x_ref, g_ref, w_ref, o_ref, *, tn):
    h = _rms(x_ref[...], g_ref[...]).astype(MXU_DTYPE)
    for c in range(0, w_ref.shape[1], tn):
        o_ref[:, c:c + tn] = jnp.dot(h, w_ref[:, c:c + tn], preferred_element_type=F32)


def _norm_matmul(x, g, w, tm):
    t, d = x.shape
    n = w.shape[1]
    return pl.pallas_call(
        functools.partial(_norm_matmul_kernel, tn=_col_tile(n)),
        grid=(t // tm,),
        in_specs=[pl.BlockSpec((tm, d), lambda i: (i, 0)),
                  pl.BlockSpec((1, d), lambda i: (0, 0)),
                  pl.BlockSpec((d, n), lambda i: (0, 0))],
        out_specs=pl.BlockSpec((tm, n), lambda i: (i, 0)),
        out_shape=jax.ShapeDtypeStruct((t, n), F32),
        compiler_params=pltpu.CompilerParams(dimension_semantics=("parallel",),
                                             vmem_limit_bytes=VMEM_LIMIT),
        name="norm_matmul",
    )(x, g.reshape(1, d), w)


def _split(x):
    hi = x.astype(MXU_DTYPE)
    lo = (x - hi.astype(F32)).astype(MXU_DTYPE)
    return hi, lo


def _dot_split(ah, al, bh, bl):
    return jnp.dot(jnp.concatenate([ah, ah, al], axis=1), jnp.concatenate([bh, bl, bh], axis=0),
                   preferred_element_type=F32)


GDN_GROUP = 4


def _gdn_kernel(q_ref, k_ref, v_ref, gate_ref, ba_ref, cwq_ref, cwk_ref, cwv_ref,
                alog_ref, dtb_ref, ogain_ref, o_ref,
                qn_s, kn_s, vv_s, beta_s, gc_s, u_s, oo_s, lhs_s, b_s, a_s, cd_s,
                *, n_heads):
    head = pl.program_id(1)
    seq = q_ref.shape[0]
    dh = GDN_HEAD_DIM
    ch = GDN_CHUNK
    blk = 2 * ch
    grp = min(GDN_GROUP, seq // blk)
    ngrp = seq // (blk * grp)

    row = lax.broadcasted_iota(jnp.int32, (seq, dh), 0)

    def conv_silu(x_ref, cw_ref):
        x = x_ref[...]
        cw = cw_ref[...]
        y = x * cw[CONV_WIDTH - 1:CONV_WIDTH, :]
        for sh in range(1, CONV_WIDTH):
            xs = jnp.where(row >= sh, pltpu.roll(x, sh, 0), 0.0)
            y = y + xs * cw[CONV_WIDTH - 1 - sh:CONV_WIDTH - sh, :]
        return y * _sigmoid(y)

    def l2n(x):
        return x * lax.rsqrt(jnp.sum(x * x, axis=-1, keepdims=True) + EPS)

    qn_s[...] = l2n(conv_silu(q_ref, cwq_ref)) * (dh ** -0.5)
    kn_s[...] = l2n(conv_silu(k_ref, cwk_ref))
    vv_s[...] = conv_silu(v_ref, cwv_ref)

    ba = ba_ref[...]
    lane = lax.broadcasted_iota(jnp.int32, (seq, LANES), 1)
    beta_all = _sigmoid(ba)
    g_all = -jnp.exp(alog_ref[...]) * _softplus(ba + dtb_ref[...])
    beta_col = jnp.sum(jnp.where(lane == head, beta_all, 0.0), axis=-1, keepdims=True)
    g_col = jnp.sum(jnp.where(lane == head + n_heads, g_all, 0.0), axis=-1, keepdims=True)
    beta_s[...] = jnp.broadcast_to(beta_col, (seq, dh))
    gc = jnp.broadcast_to(g_col, (seq, dh))
    pos = row & (ch - 1)
    sh = 1
    while sh < ch:
        gc = gc + jnp.where(pos >= sh, pltpu.roll(gc, sh, 0), 0.0)
        sh *= 2
    gc_s[...] = gc

    ri = lax.broadcasted_iota(jnp.int32, (blk, blk), 0)
    ci = lax.broadcasted_iota(jnp.int32, (blk, blk), 1)
    same = (ri // ch) == (ci // ch)
    causal = jnp.logical_and(same, ri >= ci)
    strict = jnp.logical_and(same, ri > ci)
    eye = jnp.where(ri == ci, 1.0, 0.0).astype(F32)
    first = ri < ch

    def group_prep(jg, _):
        idx = [jg * grp + g for g in range(grp)]
        sls = [pl.ds(pl.multiple_of(i * blk, blk), blk) for i in idx]
        q = [qn_s[s, :] for s in sls]
        k = [kn_s[s, :] for s in sls]
        v = [vv_s[s, :] for s in sls]
        bb = [beta_s[s, :] for s in sls]
        gcb = [gc_s[s, :] for s in sls]
        decay = [jnp.exp(jnp.where(causal, x - x.T, -1e30)) for x in gcb]
        egc = [jnp.exp(x) for x in gcb]
        kb = [a * b for a, b in zip(k, bb)]
        kq = [_dot_nt(jnp.concatenate([a, b], axis=0), c) for a, b, c in zip(kb, q, k)]
        low = [jnp.where(strict, a[:blk] * d, 0.0) for a, d in zip(kq, decay)]
        inv = [eye - x for x in low]
        pw = [_split(x) for x in low]
        n = 2
        while n < ch + 1:
            pw = [_split(_dot_split(h, l, h, l)) for h, l in pw]
            isp = [_split(x) for x in inv]
            inv = [x + _dot_split(ih, il, h, l) for x, (ih, il), (h, l) in zip(inv, isp, pw)]
            n *= 2
        rhs = [_split(jnp.concatenate([a * b, c * e], axis=-1)) for a, b, c, e in zip(v, bb, kb, egc)]
        isp = [_split(x) for x in inv]
        sol = [_dot_split(ih, il, rh, rl) for (ih, il), (rh, rl) in zip(isp, rhs)]
        for g in range(grp):
            gl = [gcb[g][ch - 1:ch, :], gcb[g][blk - 1:blk, :]]
            kd = k[g] * jnp.exp(jnp.where(first, gl[0], gl[1]) - gcb[g])
            qd = q[g] * egc[g]
            intra = kq[g][blk:] * decay[g]
            u_s[sls[g], :] = sol[g][:, :dh]
            for half in range(2):
                rs = slice(half * ch, (half + 1) * ch)
                c = 2 * idx[g] + half
                mb = _dot_tn(kd[rs], sol[g][rs])
                lhs_s[c] = jnp.concatenate([mb[:, dh:], sol[g][rs, dh:], qd[rs]], axis=0).astype(MXU_DTYPE)
                b_s[c] = mb[:, :dh]
                a_s[c] = intra[rs, rs].astype(MXU_DTYPE)
                cd_s[c] = jnp.exp(gl[half])
        return 0

    lax.fori_loop(0, ngrp, group_prep, 0)

    def chunk_scan(c, state):
        rows = pl.ds(pl.multiple_of(c * ch, ch), ch)
        r = jnp.dot(lhs_s[c], state.astype(MXU_DTYPE), preferred_element_type=F32)
        v_new = u_s[rows, :] - r[dh:dh + ch]
        oo_s[rows, :] = r[dh + ch:] + _dot(a_s[c], v_new)
        return state * cd_s[c] + b_s[c] - r[:dh]

    lax.fori_loop(0, seq // ch, chunk_scan, jnp.zeros((dh, dh), F32), unroll=2)

    o = oo_s[...]
    gate = gate_ref[...]
    o = o * lax.rsqrt(jnp.mean(o * o, axis=-1, keepdims=True) + EPS) * ogain_ref[...]
    o_ref[...] = o * (gate * _sigmoid(gate))


def _gdn(proj, conv_w, a_log, dt_bias, o_gain, batch, seq, n_heads):
    dh = GDN_HEAD_DIM
    width = n_heads * dh
    ba_blk = proj.shape[1] // LANES - 1
    pad = LANES - 2 * n_heads
    alog_row = jnp.concatenate([jnp.zeros((n_heads,), F32), a_log, jnp.zeros((pad,), F32)]).reshape(1, LANES)
    dtb_row = jnp.concatenate([jnp.zeros((n_heads,), F32), dt_bias, jnp.zeros((pad,), F32)]).reshape(1, LANES)
    col = lambda off: pl.BlockSpec((seq, dh), lambda b, h: (b, h + off))
    cw = lambda off: pl.BlockSpec((CONV_WIDTH, dh), lambda b, h: (0, h + off))
    row = pl.BlockSpec((1, LANES), lambda b, h: (0, 0))
    big = pltpu.VMEM((seq, dh), F32)
    nc = seq // GDN_CHUNK
    return pl.pallas_call(
        functools.partial(_gdn_kernel, n_heads=n_heads),
        grid=(batch, n_heads),
        in_specs=[col(0), col(n_heads), col(2 * n_heads), col(3 * n_heads),
                  pl.BlockSpec((seq, LANES), lambda b, h: (b, ba_blk)),
                  cw(0), cw(n_heads), cw(2 * n_heads), row, row, row],
        out_specs=pl.BlockSpec((seq, dh), lambda b, h: (b, h)),
        out_shape=jax.ShapeDtypeStruct((batch * seq, width), F32),
        scratch_shapes=[big, big, big, big, big, big, big,
                        pltpu.VMEM((nc, 2 * dh, dh), MXU_DTYPE),
                        pltpu.VMEM((nc, dh, dh), F32),
                        pltpu.VMEM((nc, GDN_CHUNK, GDN_CHUNK), MXU_DTYPE),
                        pltpu.VMEM((nc, 1, dh), F32)],
        compiler_params=pltpu.CompilerParams(dimension_semantics=("parallel", "parallel"),
                                             vmem_limit_bytes=VMEM_LIMIT),
        name="gated_deltanet",
    )(proj, proj, proj, proj, proj, conv_w, conv_w, conv_w, alog_row, dtb_row,
      o_gain.reshape(1, dh))


SB_KEY_TILE = 256
SB_SUBTILES = 2


def _sb_kernel(q_ref, k_ref, v_ref, o_ref, pre_s, rs_s, acc_s, carry_s):
    qi = pl.program_id(2)
    tq = q_ref.shape[0]
    tk = SB_KEY_TILE
    nsub = SB_SUBTILES
    nh = LANES // SB_HEAD_DIM
    lane_q = lax.broadcasted_iota(jnp.int32, (tq, LANES), 1)
    ri = lax.broadcasted_iota(jnp.int32, (tq, tk), 0)
    ci = lax.broadcasted_iota(jnp.int32, (tq, tk), 1)
    after = jnp.where(lax.broadcasted_iota(jnp.int32, (tk, tk), 0) >= lax.broadcasted_iota(jnp.int32, (tk, tk), 1),
                      1.0, 0.0).astype(MXU_DTYPE)

    q_all = q_ref[...] * (SB_HEAD_DIM ** -0.5)
    qs = [jnp.where((lane_q // SB_HEAD_DIM) == h, q_all, 0.0).astype(MXU_DTYPE) for h in range(nh)]

    def score_stage(kstart, slot, diag):
        for h in range(nh):
            for s in range(nsub):
                ks = pl.ds(pl.multiple_of(kstart + s * tk, tk), tk)
                z = _dot_nt(qs[h], k_ref[ks, :])
                p = jnp.maximum(z, 0.0) + jnp.log(1.0 + jnp.exp2(jnp.abs(z) * (-LOG2E)))
                if diag:
                    before = (ci + s * tk) < ri
                    p = jnp.where(before, p, 0.0)
                t = jnp.dot(p.astype(MXU_DTYPE), after, preferred_element_type=F32)
                pre = z - t
                if diag:
                    pre = jnp.where(before, pre, -1e30)
                pre_s[slot, h, :, s * tk:(s + 1) * tk] = pre
                rs_s[slot, h * nsub + s] = t[:, 0:1]

    def value_stage(kstart, slot):
        for h in range(nh):
            carry = carry_s[h]
            acc = jnp.zeros((tq, LANES), F32)
            for s in reversed(range(nsub)):
                ks = pl.ds(pl.multiple_of(kstart + s * tk, tk), tk)
                a = jnp.exp(pre_s[slot, h, :, s * tk:(s + 1) * tk] - carry)
                acc = acc + _dot(a, v_ref[ks, :])
                carry = carry + rs_s[slot, h * nsub + s]
            carry_s[h] = carry
            acc_s[h] = acc_s[h] + acc

    acc_s[...] = jnp.zeros(acc_s.shape, F32)
    carry_s[...] = jnp.zeros(carry_s.shape, F32)
    score_stage(qi * tq, 0, True)

    def body(i, _):
        value_stage((qi - i) * tq, i & 1)
        score_stage((qi - 1 - i) * tq, (i + 1) & 1, False)
        return 0

    lax.fori_loop(0, qi, body, 0)
    value_stage(0, qi & 1)
    out = acc_s[0]
    for h in range(1, nh):
        out = jnp.where((lane_q // SB_HEAD_DIM) == h, acc_s[h], out)
    o_ref[...] = out


def _stick_breaking(proj, batch, seq, width):
    tq = SB_KEY_TILE * SB_SUBTILES
    npair = width // LANES
    nq = seq // tq
    nh = LANES // SB_HEAD_DIM
    return pl.pallas_call(
        _sb_kernel,
        grid=(batch, npair, nq),
        in_specs=[pl.BlockSpec((tq, LANES), lambda b, p, i: (b * nq + i, p)),
                  pl.BlockSpec((seq, LANES), lambda b, p, i: (b, npair + p)),
                  pl.BlockSpec((seq, LANES), lambda b, p, i: (b, 2 * npair + p))],
        out_specs=pl.BlockSpec((tq, LANES), lambda b, p, i: (b * nq + i, p)),
        out_shape=jax.ShapeDtypeStruct((batch * seq, width), F32),
        scratch_shapes=[pltpu.VMEM((2, nh, tq, tq), F32),
                        pltpu.VMEM((2, nh * SB_SUBTILES, tq, 1), F32),
                        pltpu.VMEM((nh, tq, LANES), F32),
                        pltpu.VMEM((nh, tq, 1), F32)],
        compiler_params=pltpu.CompilerParams(
            dimension_semantics=("parallel", "parallel", "arbitrary"),
            vmem_limit_bytes=VMEM_LIMIT),
        name="stick_breaking",
    )(proj, proj, proj)


def _post_mix_kernel(x_ref, mix_ref, mq_ref, mk_ref, mv_ref, wo_ref, g_ref, o_ref):
    dh = X_HEAD_DIM
    mq = mq_ref[...]
    mk = mk_ref[...]
    mv = mv_ref[...]
    heads = []
    for h in range(X_HEADS):
        cols = slice(h * dh, (h + 1) * dh)
        s = _dot_nt(mq[:, cols] * (dh ** -0.5), mk[:, cols])
        p = jnp.exp(s - jnp.max(s, axis=-1, keepdims=True))
        inv_l = 1.0 / jnp.sum(p, axis=-1, keepdims=True)
        heads.append(_dot(p * inv_l, mv[:, cols]))
    cross = jnp.concatenate(heads, axis=-1)
    wmix = mix_ref.shape[1]
    y = _dot(mix_ref[...], wo_ref[:wmix, :]) + _dot(cross, wo_ref[wmix:, :])
    o_ref[...] = x_ref[...] + _rms(y, g_ref[...])


def _post_mix(x, mix, proj, memq_blk, mem_kv, layer, wo, g, batch, seq, tm):
    t, d = x.shape
    wmix = mix.shape[1]
    xw = X_HEADS * X_HEAD_DIM
    nt = seq // tm
    return pl.pallas_call(
        _post_mix_kernel,
        grid=(batch, nt),
        in_specs=[pl.BlockSpec((tm, d), lambda b, i: (b * nt + i, 0)),
                  pl.BlockSpec((tm, wmix), lambda b, i: (b * nt + i, 0)),
                  pl.BlockSpec((tm, xw), lambda b, i: (b * nt + i, memq_blk)),
                  pl.BlockSpec((N_MEM, xw), lambda b, i: (b, 2 * layer)),
                  pl.BlockSpec((N_MEM, xw), lambda b, i: (b, 2 * layer + 1)),
                  pl.BlockSpec((wmix + xw, d), lambda b, i: (0, 0)),
                  pl.BlockSpec((1, d), lambda b, i: (0, 0))],
        out_specs=pl.BlockSpec((tm, d), lambda b, i: (b * nt + i, 0)),
        out_shape=jax.ShapeDtypeStruct((t, d), F32),
        compiler_params=pltpu.CompilerParams(dimension_semantics=("parallel", "parallel"),
                                             vmem_limit_bytes=VMEM_LIMIT),
        name="post_mix",
    )(x, mix, proj, mem_kv, mem_kv, wo, g.reshape(1, d))


def _mlp_kernel(x_ref, g1_ref, wu_ref, wd_ref, g2_ref, o_ref, *, tf):
    x = x_ref[...]
    h = _rms(x, g1_ref[...]).astype(MXU_DTYPE)
    acc = jnp.zeros(x.shape, F32)
    for c in range(0, wu_ref.shape[1], tf):
        u = jnp.dot(h, wu_ref[:, c:c + tf], preferred_element_type=F32)
        u = jnp.square(jnp.maximum(u, 0.0)).astype(MXU_DTYPE)
        acc = acc + jnp.dot(u, wd_ref[c:c + tf, :], preferred_element_type=F32)
    o_ref[...] = x + _rms(acc, g2_ref[...])


def _mlp(x, g1, wu, wd, g2, tm, tf=1024):
    t, d = x.shape
    f = wu.shape[1]
    return pl.pallas_call(
        functools.partial(_mlp_kernel, tf=tf),
        grid=(t // tm,),
        in_specs=[pl.BlockSpec((tm, d), lambda i: (i, 0)),
                  pl.BlockSpec((1, d), lambda i: (0, 0)),
                  pl.BlockSpec((d, f), lambda i: (0, 0)),
                  pl.BlockSpec((f, d), lambda i: (0, 0)),
                  pl.BlockSpec((1, d), lambda i: (0, 0))],
        out_specs=pl.BlockSpec((tm, d), lambda i: (i, 0)),
        out_shape=jax.ShapeDtypeStruct((t, d), F32),
        compiler_params=pltpu.CompilerParams(dimension_semantics=("parallel",),
                                             vmem_limit_bytes=VMEM_LIMIT),
        name="mlp",
    )(x, g1.reshape(1, d), wu, wd, g2.reshape(1, d))


def kernel(x, mem, mem_norm, norm_pre_mix, norm_post_mix, norm_pre_mlp, norm_post_mlp,
           w_in_a, conv_w_a, a_log_a, dt_bias_a, onorm_a, w_in_b, w_mem_kv, w_out,
           w_up, w_down):
    batch, seq, d = x.shape
    depth = norm_pre_mix.shape[0]
    n_lin = a_log_a.shape[1]
    wseq = n_lin * GDN_HEAD_DIM
    xw = X_HEADS * X_HEAD_DIM
    tm = min(512, seq)
    wt = MXU_DTYPE

    xf = x.reshape(batch * seq, d)
    w_kv = jnp.concatenate([w_mem_kv[i] for i in range(depth)], axis=1).astype(wt)
    mem_kv = _norm_matmul(mem.reshape(batch * N_MEM, d), mem_norm, w_kv, tm=min(512, batch * N_MEM))

    for i in range(depth):
        j = i // 2
        if i % 2 == 0:
            w = w_in_a[j]
            n_main = 4 * wseq
            ba_pad = jnp.zeros((d, LANES - 2 * n_lin), w.dtype)
            w = jnp.concatenate([w[:, :n_main], w[:, n_main + 2 * n_lin:],
                                 w[:, n_main:n_main + 2 * n_lin], ba_pad], axis=1).astype(wt)
            proj = _norm_matmul(xf, norm_pre_mix[i], w, tm=tm)
            mix = _gdn(proj, conv_w_a[j], a_log_a[j], dt_bias_a[j], onorm_a[j], batch, seq, n_lin)
            memq_blk = n_main // xw
        else:
            proj = _norm_matmul(xf, norm_pre_mix[i], w_in_b[j].astype(wt), tm=tm)
            mix = _stick_breaking(proj, batch, seq, wseq)
            memq_blk = 3 * wseq // xw
        xf = _post_mix(xf, mix, proj, memq_blk, mem_kv, i, w_out[i].astype(wt), norm_post_mix[i],
                       batch, seq, tm)
        xf = _mlp(xf, norm_pre_mlp[i], w_up[i].astype(wt), w_down[i].astype(wt), norm_post_mlp[i], tm)
    return xf.reshape(batch, seq, d)
```

```python
import functools

import jax
import jax.numpy as jnp
from jax import lax
from jax.experimental import pallas as pl
from jax.experimental.pallas import tpu as pltpu

EPS = 1e-6
LANES = 128
CONV_WIDTH = 4
GDN_HEAD_DIM = 128
GDN_CHUNK = 64
SB_HEAD_DIM = 64
X_HEADS = 4
X_HEAD_DIM = 64
N_MEM = 256
VMEM_LIMIT = 56 * 1024 * 1024

MXU_DTYPE = jnp.bfloat16
F32 = jnp.float32
LOG2E = 1.4426950408889634


def _dot(a, b):
    return jnp.dot(a.astype(MXU_DTYPE), b.astype(MXU_DTYPE), preferred_element_type=F32)


def _dot_nt(a, b):
    return lax.dot_general(a.astype(MXU_DTYPE), b.astype(MXU_DTYPE),
                           (((1,), (1,)), ((), ())), preferred_element_type=F32)


def _dot_tn(a, b):
    return lax.dot_general(a.astype(MXU_DTYPE), b.astype(MXU_DTYPE),
                           (((0,), (0,)), ((), ())), preferred_element_type=F32)


def _rms(x, g):
    return x * lax.rsqrt(jnp.mean(x * x, axis=-1, keepdims=True) + EPS) * g


def _sigmoid(x):
    return 1.0 / (1.0 + jnp.exp(-x))


def _softplus(x):
    return jnp.maximum(x, 0.0) + jnp.log(1.0 + jnp.exp(-jnp.abs(x)))


MATMUL_N_CHUNK = 512


def _norm_matmul_kernel(x_ref, g_ref, w_ref, o_ref):
    h = _rms(x_ref[...], g_ref[...]).astype(MXU_DTYPE)
    n = w_ref.shape[1]
    for c in range(0, n, MATMUL_N_CHUNK):
        e = min(c + MATMUL_N_CHUNK, n)
        o_ref[:, c:e] = jnp.dot(h, w_ref[:, c:e], preferred_element_type=F32)


def _norm_matmul(x, g, w, tm):
    t, d = x.shape
    n = w.shape[1]
    return pl.pallas_call(
        _norm_matmul_kernel,
        grid=(t // tm,),
        in_specs=[pl.BlockSpec((tm, d), lambda i: (i, 0)),
                  pl.BlockSpec((1, d), lambda i: (0, 0)),
                  pl.BlockSpec((d, n), lambda i: (0, 0))],
        out_specs=pl.BlockSpec((tm, n), lambda i: (i, 0)),
        out_shape=jax.ShapeDtypeStruct((t, n), F32),
        compiler_params=pltpu.CompilerParams(dimension_semantics=("parallel",),
                                             vmem_limit_bytes=VMEM_LIMIT),
        name="norm_matmul",
    )(x, g.reshape(1, d), w)


def _split(x):
    hi = x.astype(MXU_DTYPE)
    lo = (x - hi.astype(F32)).astype(MXU_DTYPE)
    return hi, lo


def _dot_split(ah, al, bh, bl):
    return jnp.dot(jnp.concatenate([ah, ah, al], axis=1), jnp.concatenate([bh, bl, bh], axis=0),
                   preferred_element_type=F32)


GDN_GROUP = 8
GDN_HEADS_PER_STEP = 2


def _gdn_kernel(q_ref, k_ref, v_ref, gate_ref, ba_ref, cwq_ref, cwk_ref, cwv_ref,
                alog_ref, dtb_ref, ogain_ref, o_ref,
                bsig_s, gall_s, qn_s, kn_s, vv_s, beta_s, gc_s, u_s, oo_s, lhs_s, b_s, a_s, cd_s,
                *, n_heads):
    seq = q_ref.shape[0]
    dh = GDN_HEAD_DIM
    ch = GDN_CHUNK
    hps = q_ref.shape[1] // dh
    ba = ba_ref[...]
    bsig_s[...] = _sigmoid(ba)
    gall_s[...] = -jnp.exp(alog_ref[...]) * _softplus(ba + dtb_ref[...])
    for hh in range(hps):
        _gdn_prepare(hh, pl.program_id(1) * hps + hh, q_ref, k_ref, v_ref, cwq_ref, cwk_ref, cwv_ref,
                     bsig_s, gall_s, qn_s, kn_s, vv_s, beta_s, gc_s, u_s, lhs_s, b_s, a_s, cd_s,
                     n_heads=n_heads)

    def chunk_scan(c, states):
        rows = pl.ds(pl.multiple_of(c * ch, ch), ch)
        out = []
        for hh in range(hps):
            r = jnp.dot(lhs_s[hh, c], states[hh].astype(MXU_DTYPE), preferred_element_type=F32)
            v_new = u_s[hh, rows, :] - r[dh:dh + ch]
            oo_s[hh, rows, :] = r[dh + ch:] + _dot(a_s[hh, c], v_new)
            out.append(states[hh] * cd_s[hh, c] + b_s[hh, c] - r[:dh])
        return tuple(out)

    lax.fori_loop(0, seq // ch, chunk_scan, tuple(jnp.zeros((dh, dh), F32) for _ in range(hps)))

    for hh in range(hps):
        cols = slice(hh * dh, (hh + 1) * dh)
        o = oo_s[hh]
        gate = gate_ref[:, cols]
        o = o * lax.rsqrt(jnp.mean(o * o, axis=-1, keepdims=True) + EPS) * ogain_ref[...]
        o_ref[:, cols] = o * (gate * _sigmoid(gate))


def _gdn_prepare(hh, head, q_ref, k_ref, v_ref, cwq_ref, cwk_ref, cwv_ref, bsig_s, gall_s,
                 qn_s, kn_s, vv_s, beta_s, gc_s, u_s, lhs_s, b_s, a_s, cd_s, *, n_heads):
    seq = q_ref.shape[0]
    dh = GDN_HEAD_DIM
    ch = GDN_CHUNK
    blk = 2 * ch
    grp = min(GDN_GROUP, seq // blk)
    ngrp = seq // (blk * grp)
    cols = slice(hh * dh, (hh + 1) * dh)

    row = lax.broadcasted_iota(jnp.int32, (seq, dh), 0)
    row8 = lax.broadcasted_iota(jnp.int32, (8, dh), 0)

    def conv_silu(x_ref, cw_ref):
        cw = cw_ref[:, cols]
        taps = [cw[CONV_WIDTH - 1 - sh:CONV_WIDTH - sh, :] for sh in range(CONV_WIDTH)]
        y = x_ref[8:, cols] * taps[0]
        for sh in range(1, CONV_WIDTH):
            y = y + x_ref[8 - sh:seq - sh, cols] * taps[sh]
        x0 = x_ref[:8, cols]
        y0 = x0 * taps[0]
        for sh in range(1, CONV_WIDTH):
            y0 = y0 + jnp.where(row8 >= sh, pltpu.roll(x0, sh, 0), 0.0) * taps[sh]
        y = jnp.concatenate([y0, y], axis=0)
        return y * _sigmoid(y)

    def l2n(x):
        return x * lax.rsqrt(jnp.sum(x * x, axis=-1, keepdims=True) + EPS)

    qn_s[...] = l2n(conv_silu(q_ref, cwq_ref)) * (dh ** -0.5)
    kn_s[...] = l2n(conv_silu(k_ref, cwk_ref))
    vv_s[...] = conv_silu(v_ref, cwv_ref)

    lane = lax.broadcasted_iota(jnp.int32, (seq, LANES), 1)
    beta_col = jnp.sum(jnp.where(lane == head, bsig_s[...], 0.0), axis=-1, keepdims=True)
    g_col = jnp.sum(jnp.where(lane == head + n_heads, gall_s[...], 0.0), axis=-1, keepdims=True)
    beta_s[...] = jnp.broadcast_to(beta_col, (seq, dh))
    gc = jnp.broadcast_to(g_col, (seq, dh))
    pos = row & (ch - 1)
    sh = 1
    while sh < ch:
        gc = gc + jnp.where(pos >= sh, pltpu.roll(gc, sh, 0), 0.0)
        sh *= 2
    gc_s[...] = gc

    ri = lax.broadcasted_iota(jnp.int32, (ch, blk), 0)
    ci = lax.broadcasted_iota(jnp.int32, (ch, blk), 1)
    lane_lo = ci < ch
    causal = ri >= (ci & (ch - 1))
    strict = ri > (ci & (ch - 1))
    eye = jnp.where(ri == (ci & (ch - 1)), 1.0, 0.0).astype(F32)
    first = lax.broadcasted_iota(jnp.int32, (blk, blk), 0) < ch

    def side_by_side(x):
        return jnp.where(lane_lo, x[:ch], x[ch:])

    def block_diag(x):
        zero = jnp.zeros_like(x)
        return jnp.concatenate([jnp.where(lane_lo, x, zero), jnp.where(lane_lo, zero, x)], axis=0)

    def group_prep(jg, _):
        idx = [jg * grp + g for g in range(grp)]
        sls = [pl.ds(pl.multiple_of(i * blk, blk), blk) for i in idx]
        q = [qn_s[s, :] for s in sls]
        k = [kn_s[s, :] for s in sls]
        v = [vv_s[s, :] for s in sls]
        bb = [beta_s[s, :] for s in sls]
        gcb = [gc_s[s, :] for s in sls]
        gcr = [jnp.where(lane_lo, x[:ch], x[ch:]) for x in gcb]
        decay = [jnp.exp(jnp.where(causal, a - x.T[:ch], -1e30)) for a, x in zip(gcr, gcb)]
        egc = [jnp.exp(x) for x in gcb]
        kb = [a * b for a, b in zip(k, bb)]
        kq = [_dot_nt(jnp.concatenate([a, b], axis=0), c) for a, b, c in zip(kb, q, k)]
        low = [jnp.where(strict, side_by_side(a[:blk]) * d, 0.0) for a, d in zip(kq, decay)]
        intra = [side_by_side(a[blk:]) * d for a, d in zip(kq, decay)]
        inv = [eye - x for x in low]
        pw = [_split(x) for x in low]
        n = 2
        while n < ch + 1:
            pw = [_split(_dot_split(h, l, block_diag(h), block_diag(l))) for h, l in pw]
            isp = [_split(x) for x in inv]
            inv = [x + _dot_split(ih, il, block_diag(h), block_diag(l)) for x, (ih, il), (h, l) in zip(inv, isp, pw)]
            n *= 2
        rhs = [_split(jnp.concatenate([a * b, c * e], axis=-1)) for a, b, c, e in zip(v, bb, kb, egc)]
        isp = [_split(x) for x in inv]
        sol = [_dot_split(block_diag(ih), block_diag(il), rh, rl) for (ih, il), (rh, rl) in zip(isp, rhs)]
        for g in range(grp):
            gl = [gcb[g][ch - 1:ch, :], gcb[g][blk - 1:blk, :]]
            kd = k[g] * jnp.exp(jnp.where(first, gl[0], gl[1]) - gcb[g])
            qd = q[g] * egc[g]
            u_s[hh, sls[g], :] = sol[g][:, :dh]
            for half in range(2):
                rs = slice(half * ch, (half + 1) * ch)
                c = 2 * idx[g] + half
                mb = _dot_tn(kd[rs], sol[g][rs])
                lhs_s[hh, c] = jnp.concatenate([mb[:, dh:], sol[g][rs, dh:], qd[rs]], axis=0).astype(MXU_DTYPE)
                b_s[hh, c] = mb[:, :dh]
                a_s[hh, c] = intra[g][:, rs].astype(MXU_DTYPE)
                cd_s[hh, c] = jnp.exp(gl[half])
        return 0

    lax.fori_loop(0, ngrp, group_prep, 0)


def _gdn(proj, conv_w, a_log, dt_bias, o_gain, batch, seq, n_heads):
    dh = GDN_HEAD_DIM
    width = n_heads * dh
    ba_blk = proj.shape[1] // LANES - 1
    pad = LANES - 2 * n_heads
    alog_row = jnp.concatenate([jnp.zeros((n_heads,), F32), a_log, jnp.zeros((pad,), F32)]).reshape(1, LANES)
    dtb_row = jnp.concatenate([jnp.zeros((n_heads,), F32), dt_bias, jnp.zeros((pad,), F32)]).reshape(1, LANES)
    hps = GDN_HEADS_PER_STEP
    nblk = n_heads // hps
    col = lambda off: pl.BlockSpec((seq, hps * dh), lambda b, h: (b, h + off))
    cw = lambda off: pl.BlockSpec((CONV_WIDTH, hps * dh), lambda b, h: (0, h + off))
    row = pl.BlockSpec((1, LANES), lambda b, h: (0, 0))
    big = pltpu.VMEM((seq, dh), F32)
    per_head = pltpu.VMEM((hps, seq, dh), F32)
    nc = seq // GDN_CHUNK
    return pl.pallas_call(
        functools.partial(_gdn_kernel, n_heads=n_heads),
        grid=(batch, nblk),
        in_specs=[col(0), col(nblk), col(2 * nblk), col(3 * nblk),
                  pl.BlockSpec((seq, LANES), lambda b, h: (b, ba_blk)),
                  cw(0), cw(nblk), cw(2 * nblk), row, row, row],
        out_specs=pl.BlockSpec((seq, hps * dh), lambda b, h: (b, h)),
        out_shape=jax.ShapeDtypeStruct((batch * seq, width), F32),
        scratch_shapes=[big, big, big, big, big, big, big, per_head, per_head,
                        pltpu.VMEM((hps, nc, 2 * dh, dh), MXU_DTYPE),
                        pltpu.VMEM((hps, nc, dh, dh), F32),
                        pltpu.VMEM((hps, nc, GDN_CHUNK, GDN_CHUNK), MXU_DTYPE),
                        pltpu.VMEM((hps, nc, 1, dh), F32)],
        compiler_params=pltpu.CompilerParams(dimension_semantics=("parallel", "parallel"),
                                             vmem_limit_bytes=VMEM_LIMIT),
        name="gated_deltanet",
    )(proj, proj, proj, proj, proj, conv_w, conv_w, conv_w, alog_row, dtb_row,
      o_gain.reshape(1, dh))


SB_KEY_TILE = 256
SB_SUBTILES = 2


def _sb_kernel(q_ref, k_ref, v_ref, o_ref, pre_s, rs_s, acc_s, carry_s):
    qi = pl.program_id(2)
    tq = q_ref.shape[0]
    tk = SB_KEY_TILE
    nsub = SB_SUBTILES
    nh = LANES // SB_HEAD_DIM
    lane_q = lax.broadcasted_iota(jnp.int32, (tq, LANES), 1)
    after = jnp.where(lax.broadcasted_iota(jnp.int32, (tk, tk), 0) >= lax.broadcasted_iota(jnp.int32, (tk, tk), 1),
                      1.0, 0.0).astype(MXU_DTYPE)

    q_all = q_ref[...] * (SB_HEAD_DIM ** -0.5)
    qs = [jnp.where((lane_q // SB_HEAD_DIM) == h, q_all, 0.0).astype(MXU_DTYPE) for h in range(nh)]

    def score_stage(kstart, slot, diag):
        for h in range(nh):
            for s in range(nsub):
                ks = pl.ds(pl.multiple_of(kstart + s * tk, tk), tk)
                r0 = s * tk if diag else 0
                z = _dot_nt(qs[h][r0:], k_ref[ks, :])
                p = jnp.maximum(z, 0.0) + jnp.log(1.0 + jnp.exp2(jnp.abs(z) * (-LOG2E)))
                if diag:
                    shape = (tq - r0, tk)
                    before = lax.broadcasted_iota(jnp.int32, shape, 1) < lax.broadcasted_iota(jnp.int32, shape, 0)
                    p = jnp.where(before, p, 0.0)
                t = jnp.dot(p.astype(MXU_DTYPE), after, preferred_element_type=F32)
                pre = z - t
                if diag:
                    pre = jnp.where(before, pre, -1e30)
                pre_s[slot, h, r0:, s * tk:(s + 1) * tk] = pre
                rs_s[slot, h * nsub + s, r0:] = t[:, 0:1]
                if r0:
                    pre_s[slot, h, :r0, s * tk:(s + 1) * tk] = jnp.full((r0, tk), -1e30, F32)
                    rs_s[slot, h * nsub + s, :r0] = jnp.zeros((r0, 1), F32)

    def value_stage(kstart, slot):
        for h in range(nh):
            carry = carry_s[h]
            acc = jnp.zeros((tq, LANES), F32)
            for s in reversed(range(nsub)):
                ks = pl.ds(pl.multiple_of(kstart + s * tk, tk), tk)
                a = jnp.exp(pre_s[slot, h, :, s * tk:(s + 1) * tk] - carry)
                acc = acc + _dot(a, v_ref[ks, :])
                carry = carry + rs_s[slot, h * nsub + s]
            carry_s[h] = carry
            acc_s[h] = acc_s[h] + acc

    acc_s[...] = jnp.zeros(acc_s.shape, F32)
    carry_s[...] = jnp.zeros(carry_s.shape, F32)
    score_stage(qi * tq, 0, True)

    def body(i, _):
        value_stage((qi - i) * tq, i & 1)
        score_stage((qi - 1 - i) * tq, (i + 1) & 1, False)
        return 0

    lax.fori_loop(0, qi, body, 0)
    value_stage(0, qi & 1)
    out = acc_s[0]
    for h in range(1, nh):
        out = jnp.where((lane_q // SB_HEAD_DIM) == h, acc_s[h], out)
    o_ref[...] = out


def _stick_breaking(proj, batch, seq, width):
    tq = SB_KEY_TILE * SB_SUBTILES
    npair = width // LANES
    nq = seq // tq
    nh = LANES // SB_HEAD_DIM
    return pl.pallas_call(
        _sb_kernel,
        grid=(batch, npair, nq),
        in_specs=[pl.BlockSpec((tq, LANES), lambda b, p, i: (b * nq + i, p)),
                  pl.BlockSpec((seq, LANES), lambda b, p, i: (b, npair + p)),
                  pl.BlockSpec((seq, LANES), lambda b, p, i: (b, 2 * npair + p))],
        out_specs=pl.BlockSpec((tq, LANES), lambda b, p, i: (b * nq + i, p)),
        out_shape=jax.ShapeDtypeStruct((batch * seq, width), F32),
        scratch_shapes=[pltpu.VMEM((2, nh, tq, tq), F32),
                        pltpu.VMEM((2, nh * SB_SUBTILES, tq, 1), F32),
                        pltpu.VMEM((nh, tq, LANES), F32),
                        pltpu.VMEM((nh, tq, 1), F32)],
        compiler_params=pltpu.CompilerParams(
            dimension_semantics=("parallel", "parallel", "arbitrary"),
            vmem_limit_bytes=VMEM_LIMIT),
        name="stick_breaking",
    )(proj, proj, proj)


def _post_mix_kernel(x_ref, mix_ref, mq_ref, mk_ref, mv_ref, wo_ref, g_ref, o_ref):
    dh = X_HEAD_DIM
    mq = mq_ref[...]
    mk = mk_ref[...]
    mv = mv_ref[...]
    heads = []
    for h in range(X_HEADS):
        cols = slice(h * dh, (h + 1) * dh)
        s = _dot_nt(mq[:, cols] * (dh ** -0.5), mk[:, cols])
        p = jnp.exp(s - jnp.max(s, axis=-1, keepdims=True))
        inv_l = 1.0 / jnp.sum(p, axis=-1, keepdims=True)
        heads.append(_dot(p * inv_l, mv[:, cols]))
    cross = jnp.concatenate(heads, axis=-1)
    wmix = mix_ref.shape[1]
    y = _dot(mix_ref[...], wo_ref[:wmix, :]) + _dot(cross, wo_ref[wmix:, :])
    o_ref[...] = x_ref[...] + _rms(y, g_ref[...])


def _post_mix(x, mix, proj, memq_blk, mem_kv, layer, wo, g, batch, seq, tm):
    t, d = x.shape
    wmix = mix.shape[1]
    xw = X_HEADS * X_HEAD_DIM
    nt = seq // tm
    return pl.pallas_call(
        _post_mix_kernel,
        grid=(batch, nt),
        in_specs=[pl.BlockSpec((tm, d), lambda b, i: (b * nt + i, 0)),
                  pl.BlockSpec((tm, wmix), lambda b, i: (b * nt + i, 0)),
                  pl.BlockSpec((tm, xw), lambda b, i: (b * nt + i, memq_blk)),
                  pl.BlockSpec((N_MEM, xw), lambda b, i: (b, 2 * layer)),
                  pl.BlockSpec((N_MEM, xw), lambda b, i: (b, 2 * layer + 1)),
                  pl.BlockSpec((wmix + xw, d), lambda b, i: (0, 0)),
                  pl.BlockSpec((1, d), lambda b, i: (0, 0))],
        out_specs=pl.BlockSpec((tm, d), lambda b, i: (b * nt + i, 0)),
        out_shape=jax.ShapeDtypeStruct((t, d), F32),
        compiler_params=pltpu.CompilerParams(dimension_semantics=("parallel", "parallel"),
                                             vmem_limit_bytes=VMEM_LIMIT),
        name="post_mix",
    )(x, mix, proj, mem_kv, mem_kv, wo, g.reshape(1, d))


def _mlp_kernel(x_ref, g1_ref, wu_ref, wd_ref, g2_ref, o_ref, *, tf):
    x = x_ref[...]
    h = _rms(x, g1_ref[...]).astype(MXU_DTYPE)
    acc = jnp.zeros(x.shape, F32)
    for c in range(0, wu_ref.shape[1], tf):
        u = jnp.dot(h, wu_ref[:, c:c + tf], preferred_element_type=F32)
        u = jnp.square(jnp.maximum(u, 0.0)).astype(MXU_DTYPE)
        acc = acc + jnp.dot(u, wd_ref[c:c + tf, :], preferred_element_type=F32)
    o_ref[...] = x + _rms(acc, g2_ref[...])


def _mlp(x, g1, wu, wd, g2, tm, tf=1024):
    t, d = x.shape
    f = wu.shape[1]
    return pl.pallas_call(
        functools.partial(_mlp_kernel, tf=tf),
        grid=(t // tm,),
        in_specs=[pl.BlockSpec((tm, d), lambda i: (i, 0)),
                  pl.BlockSpec((1, d), lambda i: (0, 0)),
                  pl.BlockSpec((d, f), lambda i: (0, 0)),
                  pl.BlockSpec((f, d), lambda i: (0, 0)),
                  pl.BlockSpec((1, d), lambda i: (0, 0))],
        out_specs=pl.BlockSpec((tm, d), lambda i: (i, 0)),
        out_shape=jax.ShapeDtypeStruct((t, d), F32),
        compiler_params=pltpu.CompilerParams(dimension_semantics=("parallel",),
                                             vmem_limit_bytes=VMEM_LIMIT),
        name="mlp",
    )(x, g1.reshape(1, d), wu, wd, g2.reshape(1, d))


def kernel(x, mem, mem_norm, norm_pre_mix, norm_post_mix, norm_pre_mlp, norm_post_mlp,
           w_in_a, conv_w_a, a_log_a, dt_bias_a, onorm_a, w_in_b, w_mem_kv, w_out,
           w_up, w_down):
    batch, seq, d = x.shape
    depth = norm_pre_mix.shape[0]
    n_lin = a_log_a.shape[1]
    wseq = n_lin * GDN_HEAD_DIM
    xw = X_HEADS * X_HEAD_DIM
    tm = min(512, seq)
    wt = MXU_DTYPE

    xf = x.reshape(batch * seq, d)
    w_kv = jnp.concatenate([w_mem_kv[i] for i in range(depth)], axis=1).astype(wt)
    mem_kv = _norm_matmul(mem.reshape(batch * N_MEM, d), mem_norm, w_kv, tm=min(512, batch * N_MEM))

    for i in range(depth):
        j = i // 2
        if i % 2 == 0:
            w = w_in_a[j]
            n_main = 4 * wseq
            ba_pad = jnp.zeros((d, LANES - 2 * n_lin), w.dtype)
            w = jnp.concatenate([w[:, :n_main], w[:, n_main + 2 * n_lin:],
                                 w[:, n_main:n_main + 2 * n_lin], ba_pad], axis=1).astype(wt)
            proj = _norm_matmul(xf, norm_pre_mix[i], w, tm=tm)
            mix = _gdn(proj, conv_w_a[j], a_log_a[j], dt_bias_a[j], onorm_a[j], batch, seq, n_lin)
            memq_blk = n_main // xw
        else:
            proj = _norm_matmul(xf, norm_pre_mix[i], w_in_b[j].astype(wt), tm=tm)
            mix = _stick_breaking(proj, batch, seq, wseq)
            memq_blk = 3 * wseq // xw
        xf = _post_mix(xf, mix, proj, memq_blk, mem_kv, i, w_out[i].astype(wt), norm_post_mix[i],
                       batch, seq, tm)
        xf = _mlp(xf, norm_pre_mlp[i], w_up[i].astype(wt), w_down[i].astype(wt), norm_post_mlp[i], tm)
    return xf.reshape(batch, seq, d)
```

```python
import functools

import jax
import jax.numpy as jnp
from jax import lax
from jax.experimental import pallas as pl
from jax.experimental.pallas import tpu as pltpu

EPS = 1e-6
LANES = 128
CONV_WIDTH = 4
GDN_HEAD_DIM = 128
GDN_CHUNK = 64
SB_HEAD_DIM = 64
X_HEADS = 4
X_HEAD_DIM = 64
N_MEM = 256
VMEM_LIMIT = 56 * 1024 * 1024

MXU_DTYPE = jnp.bfloat16
F32 = jnp.float32
LOG2E = 1.4426950408889634


def _dot(a, b):
    return jnp.dot(a.astype(MXU_DTYPE), b.astype(MXU_DTYPE), preferred_element_type=F32)


def _dot_nt(a, b):
    return lax.dot_general(a.astype(MXU_DTYPE), b.astype(MXU_DTYPE),
                           (((1,), (1,)), ((), ())), preferred_element_type=F32)


def _dot_tn(a, b):
    return lax.dot_general(a.astype(MXU_DTYPE), b.astype(MXU_DTYPE),
                           (((0,), (0,)), ((), ())), preferred_element_type=F32)


def _rms(x, g):
    return x * lax.rsqrt(jnp.mean(x * x, axis=-1, keepdims=True) + EPS) * g


def _sigmoid(x):
    return 1.0 / (1.0 + jnp.exp(-x))


def _softplus(x):
    return jnp.maximum(x, 0.0) + jnp.log(1.0 + jnp.exp(-jnp.abs(x)))


MATMUL_N_CHUNK = 512


def _norm_matmul_kernel(x_ref, g_ref, w_ref, o_ref):
    h = _rms(x_ref[...], g_ref[...]).astype(MXU_DTYPE)
    n = w_ref.shape[1]
    for c in range(0, n, MATMUL_N_CHUNK):
        e = min(c + MATMUL_N_CHUNK, n)
        o_ref[:, c:e] = jnp.dot(h, w_ref[:, c:e], preferred_element_type=F32)


def _norm_matmul(x, g, w, tm):
    t, d = x.shape
    n = w.shape[1]
    return pl.pallas_call(
        _norm_matmul_kernel,
        grid=(t // tm,),
        in_specs=[pl.BlockSpec((tm, d), lambda i: (i, 0)),
                  pl.BlockSpec((1, d), lambda i: (0, 0)),
                  pl.BlockSpec((d, n), lambda i: (0, 0))],
        out_specs=pl.BlockSpec((tm, n), lambda i: (i, 0)),
        out_shape=jax.ShapeDtypeStruct((t, n), F32),
        compiler_params=pltpu.CompilerParams(dimension_semantics=("parallel",),
                                             vmem_limit_bytes=VMEM_LIMIT),
        name="norm_matmul",
    )(x, g.reshape(1, d), w)


def _split(x):
    hi = x.astype(MXU_DTYPE)
    lo = (x - hi.astype(F32)).astype(MXU_DTYPE)
    return hi, lo


def _dot_split(ah, al, bh, bl):
    return jnp.dot(jnp.concatenate([ah, ah, al], axis=1), jnp.concatenate([bh, bl, bh], axis=0),
                   preferred_element_type=F32)


GDN_GROUP = 8
GDN_HEADS_PER_STEP = 2


def _gdn_kernel(q_ref, k_ref, v_ref, gate_ref, ba_ref, cwq_ref, cwk_ref, cwv_ref,
                alog_ref, dtb_ref, ogain_ref, o_ref,
                bsig_s, gall_s, qn_s, kn_s, vv_s, beta_s, gc_s, u_s, oo_s, lhs_s, b_s, a_s, cd_s,
                *, n_heads):
    seq = q_ref.shape[0]
    dh = GDN_HEAD_DIM
    ch = GDN_CHUNK
    hps = q_ref.shape[1] // dh
    ba = ba_ref[...]
    bsig_s[...] = _sigmoid(ba)
    gall_s[...] = -jnp.exp(alog_ref[...]) * _softplus(ba + dtb_ref[...])
    for hh in range(hps):
        _gdn_prepare(hh, pl.program_id(1) * hps + hh, q_ref, k_ref, v_ref, cwq_ref, cwk_ref, cwv_ref,
                     bsig_s, gall_s, qn_s, kn_s, vv_s, beta_s, gc_s, u_s, lhs_s, b_s, a_s, cd_s,
                     n_heads=n_heads)

    def chunk_scan(c, states):
        rows = pl.ds(pl.multiple_of(c * ch, ch), ch)
        out = []
        for hh in range(hps):
            r = jnp.dot(lhs_s[hh, c], states[hh].astype(MXU_DTYPE), preferred_element_type=F32)
            v_new = u_s[hh, rows, :] - r[dh:dh + ch]
            oo_s[hh, rows, :] = r[dh + ch:] + _dot(a_s[hh, c], v_new)
            out.append(states[hh] * cd_s[hh, c] + b_s[hh, c] - r[:dh])
        return tuple(out)

    lax.fori_loop(0, seq // ch, chunk_scan, tuple(jnp.zeros((dh, dh), F32) for _ in range(hps)),
                  unroll=8)

    for hh in range(hps):
        cols = slice(hh * dh, (hh + 1) * dh)
        o = oo_s[hh]
        gate = gate_ref[:, cols]
        o = o * lax.rsqrt(jnp.mean(o * o, axis=-1, keepdims=True) + EPS) * ogain_ref[...]
        o_ref[:, cols] = o * (gate * _sigmoid(gate))


def _gdn_prepare(hh, head, q_ref, k_ref, v_ref, cwq_ref, cwk_ref, cwv_ref, bsig_s, gall_s,
                 qn_s, kn_s, vv_s, beta_s, gc_s, u_s, lhs_s, b_s, a_s, cd_s, *, n_heads):
    seq = q_ref.shape[0]
    dh = GDN_HEAD_DIM
    ch = GDN_CHUNK
    blk = 2 * ch
    grp = min(GDN_GROUP, seq // blk)
    ngrp = seq // (blk * grp)
    cols = slice(hh * dh, (hh + 1) * dh)

    row = lax.broadcasted_iota(jnp.int32, (seq, dh), 0)
    row8 = lax.broadcasted_iota(jnp.int32, (8, dh), 0)

    def conv_silu(x_ref, cw_ref):
        cw = cw_ref[:, cols]
        taps = [cw[CONV_WIDTH - 1 - sh:CONV_WIDTH - sh, :] for sh in range(CONV_WIDTH)]
        y = x_ref[8:, cols] * taps[0]
        for sh in range(1, CONV_WIDTH):
            y = y + x_ref[8 - sh:seq - sh, cols] * taps[sh]
        x0 = x_ref[:8, cols]
        y0 = x0 * taps[0]
        for sh in range(1, CONV_WIDTH):
            y0 = y0 + jnp.where(row8 >= sh, pltpu.roll(x0, sh, 0), 0.0) * taps[sh]
        y = jnp.concatenate([y0, y], axis=0)
        return y * _sigmoid(y)

    def l2n(x):
        return x * lax.rsqrt(jnp.sum(x * x, axis=-1, keepdims=True) + EPS)

    qn_s[...] = l2n(conv_silu(q_ref, cwq_ref)) * (dh ** -0.5)
    kn_s[...] = l2n(conv_silu(k_ref, cwk_ref))
    vv_s[...] = conv_silu(v_ref, cwv_ref)

    lane = lax.broadcasted_iota(jnp.int32, (seq, LANES), 1)
    beta_col = jnp.sum(jnp.where(lane == head, bsig_s[...], 0.0), axis=-1, keepdims=True)
    g_col = jnp.sum(jnp.where(lane == head + n_heads, gall_s[...], 0.0), axis=-1, keepdims=True)
    beta_s[...] = jnp.broadcast_to(beta_col, (seq, dh))
    gc = jnp.broadcast_to(g_col, (seq, dh))
    pos = row & (ch - 1)
    sh = 1
    while sh < ch:
        gc = gc + jnp.where(pos >= sh, pltpu.roll(gc, sh, 0), 0.0)
        sh *= 2
    gc_s[...] = gc

    ri = lax.broadcasted_iota(jnp.int32, (ch, blk), 0)
    ci = lax.broadcasted_iota(jnp.int32, (ch, blk), 1)
    lane_lo = ci < ch
    causal = ri >= (ci & (ch - 1))
    strict = ri > (ci & (ch - 1))
    eye = jnp.where(ri == (ci & (ch - 1)), 1.0, 0.0).astype(F32)
    first = lax.broadcasted_iota(jnp.int32, (blk, blk), 0) < ch

    def side_by_side(x):
        return jnp.where(lane_lo, x[:ch], x[ch:])

    def block_diag(x):
        zero = jnp.zeros_like(x)
        return jnp.concatenate([jnp.where(lane_lo, x, zero), jnp.where(lane_lo, zero, x)], axis=0)

    def group_prep(jg, _):
        idx = [jg * grp + g for g in range(grp)]
        sls = [pl.ds(pl.multiple_of(i * blk, blk), blk) for i in idx]
        q = [qn_s[s, :] for s in sls]
        k = [kn_s[s, :] for s in sls]
        v = [vv_s[s, :] for s in sls]
        bb = [beta_s[s, :] for s in sls]
        gcb = [gc_s[s, :] for s in sls]
        gcr = [jnp.where(lane_lo, x[:ch], x[ch:]) for x in gcb]
        decay = [jnp.exp(jnp.where(causal, a - x.T[:ch], -1e30)) for a, x in zip(gcr, gcb)]
        egc = [jnp.exp(x) for x in gcb]
        kb = [a * b for a, b in zip(k, bb)]
        kq = [_dot_nt(jnp.concatenate([a, b], axis=0), c) for a, b, c in zip(kb, q, k)]
        low = [jnp.where(strict, side_by_side(a[:blk]) * d, 0.0) for a, d in zip(kq, decay)]
        intra = [side_by_side(a[blk:]) * d for a, d in zip(kq, decay)]
        inv = [eye - x for x in low]
        pw = [_split(x) for x in low]
        n = 2
        while n < ch + 1:
            pw = [_split(_dot_split(h, l, block_diag(h), block_diag(l))) for h, l in pw]
            isp = [_split(x) for x in inv]
            inv = [x + _dot_split(ih, il, block_diag(h), block_diag(l)) for x, (ih, il), (h, l) in zip(inv, isp, pw)]
            n *= 2
        rhs = [_split(jnp.concatenate([a * b, c * e], axis=-1)) for a, b, c, e in zip(v, bb, kb, egc)]
        isp = [_split(x) for x in inv]
        sol = [_dot_split(block_diag(ih), block_diag(il), rh, rl) for (ih, il), (rh, rl) in zip(isp, rhs)]
        for g in range(grp):
            gl = [gcb[g][ch - 1:ch, :], gcb[g][blk - 1:blk, :]]
            kd = k[g] * jnp.exp(jnp.where(first, gl[0], gl[1]) - gcb[g])
            qd = q[g] * egc[g]
            u_s[hh, sls[g], :] = sol[g][:, :dh]
            for half in range(2):
                rs = slice(half * ch, (half + 1) * ch)
                c = 2 * idx[g] + half
                mb = _dot_tn(kd[rs], sol[g][rs])
                lhs_s[hh, c] = jnp.concatenate([mb[:, dh:], sol[g][rs, dh:], qd[rs]], axis=0).astype(MXU_DTYPE)
                b_s[hh, c] = mb[:, :dh]
                a_s[hh, c] = intra[g][:, rs].astype(MXU_DTYPE)
                cd_s[hh, c] = jnp.exp(gl[half])
        return 0

    lax.fori_loop(0, ngrp, group_prep, 0)


def _gdn(proj, conv_w, a_log, dt_bias, o_gain, batch, seq, n_heads):
    dh = GDN_HEAD_DIM
    width = n_heads * dh
    ba_blk = proj.shape[1] // LANES - 1
    pad = LANES - 2 * n_heads
    alog_row = jnp.concatenate([jnp.zeros((n_heads,), F32), a_log, jnp.zeros((pad,), F32)]).reshape(1, LANES)
    dtb_row = jnp.concatenate([jnp.zeros((n_heads,), F32), dt_bias, jnp.zeros((pad,), F32)]).reshape(1, LANES)
    hps = GDN_HEADS_PER_STEP
    nblk = n_heads // hps
    col = lambda off: pl.BlockSpec((seq, hps * dh), lambda b, h: (b, h + off))
    cw = lambda off: pl.BlockSpec((CONV_WIDTH, hps * dh), lambda b, h: (0, h + off))
    row = pl.BlockSpec((1, LANES), lambda b, h: (0, 0))
    big = pltpu.VMEM((seq, dh), F32)
    per_head = pltpu.VMEM((hps, seq, dh), F32)
    nc = seq // GDN_CHUNK
    return pl.pallas_call(
        functools.partial(_gdn_kernel, n_heads=n_heads),
        grid=(batch, nblk),
        in_specs=[col(0), col(nblk), col(2 * nblk), col(3 * nblk),
                  pl.BlockSpec((seq, LANES), lambda b, h: (b, ba_blk)),
                  cw(0), cw(nblk), cw(2 * nblk), row, row, row],
        out_specs=pl.BlockSpec((seq, hps * dh), lambda b, h: (b, h)),
        out_shape=jax.ShapeDtypeStruct((batch * seq, width), F32),
        scratch_shapes=[big, big, big, big, big, big, big, per_head, per_head,
                        pltpu.VMEM((hps, nc, 2 * dh, dh), MXU_DTYPE),
                        pltpu.VMEM((hps, nc, dh, dh), F32),
                        pltpu.VMEM((hps, nc, GDN_CHUNK, GDN_CHUNK), MXU_DTYPE),
                        pltpu.VMEM((hps, nc, 1, dh), F32)],
        compiler_params=pltpu.CompilerParams(dimension_semantics=("parallel", "parallel"),
                                             vmem_limit_bytes=VMEM_LIMIT),
        name="gated_deltanet",
    )(proj, proj, proj, proj, proj, conv_w, conv_w, conv_w, alog_row, dtb_row,
      o_gain.reshape(1, dh))


SB_KEY_TILE = 256
SB_SUBTILES = 2


def _sb_kernel(q_ref, k_ref, v_ref, o_ref, pre_s, rs_s, acc_s, carry_s):
    qi = pl.program_id(2)
    tq = q_ref.shape[0]
    tk = SB_KEY_TILE
    nsub = SB_SUBTILES
    nh = LANES // SB_HEAD_DIM
    lane_q = lax.broadcasted_iota(jnp.int32, (tq, LANES), 1)
    after = jnp.where(lax.broadcasted_iota(jnp.int32, (tk, tk), 0) >= lax.broadcasted_iota(jnp.int32, (tk, tk), 1),
                      1.0, 0.0).astype(MXU_DTYPE)

    q_all = q_ref[...] * (SB_HEAD_DIM ** -0.5)
    qs = [jnp.where((lane_q // SB_HEAD_DIM) == h, q_all, 0.0).astype(MXU_DTYPE) for h in range(nh)]

    def score_stage(kstart, slot, diag):
        for h in range(nh):
            for s in range(nsub):
                ks = pl.ds(pl.multiple_of(kstart + s * tk, tk), tk)
                r0 = s * tk if diag else 0
                z = _dot_nt(qs[h][r0:], k_ref[ks, :])
                p = jnp.maximum(z, 0.0) + jnp.log(1.0 + jnp.exp2(jnp.abs(z) * (-LOG2E)))
                if diag:
                    shape = (tq - r0, tk)
                    before = lax.broadcasted_iota(jnp.int32, shape, 1) < lax.broadcasted_iota(jnp.int32, shape, 0)
                    p = jnp.where(before, p, 0.0)
                t = jnp.dot(p.astype(MXU_DTYPE), after, preferred_element_type=F32)
                pre = z - t
                if diag:
                    pre = jnp.where(before, pre, -1e30)
                pre_s[slot, h, r0:, s * tk:(s + 1) * tk] = pre
                rs_s[slot, h * nsub + s, r0:] = t[:, 0:1]
                if r0:
                    pre_s[slot, h, :r0, s * tk:(s + 1) * tk] = jnp.full((r0, tk), -1e30, F32)
                    rs_s[slot, h * nsub + s, :r0] = jnp.zeros((r0, 1), F32)

    def value_stage(kstart, slot):
        for h in range(nh):
            carry = carry_s[h]
            acc = jnp.zeros((tq, LANES), F32)
            for s in reversed(range(nsub)):
                ks = pl.ds(pl.multiple_of(kstart + s * tk, tk), tk)
                a = jnp.exp(pre_s[slot, h, :, s * tk:(s + 1) * tk] - carry)
                acc = acc + _dot(a, v_ref[ks, :])
                carry = carry + rs_s[slot, h * nsub + s]
            carry_s[h] = carry
            acc_s[h] = acc_s[h] + acc

    acc_s[...] = jnp.zeros(acc_s.shape, F32)
    carry_s[...] = jnp.zeros(carry_s.shape, F32)
    score_stage(qi * tq, 0, True)

    def body(i, _):
        value_stage((qi - i) * tq, i & 1)
        score_stage((qi - 1 - i) * tq, (i + 1) & 1, False)
        return 0

    lax.fori_loop(0, qi, body, 0)
    value_stage(0, qi & 1)
    out = acc_s[0]
    for h in range(1, nh):
        out = jnp.where((lane_q // SB_HEAD_DIM) == h, acc_s[h], out)
    o_ref[...] = out


def _stick_breaking(proj, batch, seq, width):
    tq = SB_KEY_TILE * SB_SUBTILES
    npair = width // LANES
    nq = seq // tq
    nh = LANES // SB_HEAD_DIM
    return pl.pallas_call(
        _sb_kernel,
        grid=(batch, npair, nq),
        in_specs=[pl.BlockSpec((tq, LANES), lambda b, p, i: (b * nq + i, p)),
                  pl.BlockSpec((seq, LANES), lambda b, p, i: (b, npair + p)),
                  pl.BlockSpec((seq, LANES), lambda b, p, i: (b, 2 * npair + p))],
        out_specs=pl.BlockSpec((tq, LANES), lambda b, p, i: (b * nq + i, p)),
        out_shape=jax.ShapeDtypeStruct((batch * seq, width), F32),
        scratch_shapes=[pltpu.VMEM((2, nh, tq, tq), F32),
                        pltpu.VMEM((2, nh * SB_SUBTILES, tq, 1), F32),
                        pltpu.VMEM((nh, tq, LANES), F32),
                        pltpu.VMEM((nh, tq, 1), F32)],
        compiler_params=pltpu.CompilerParams(
            dimension_semantics=("parallel", "parallel", "arbitrary"),
            vmem_limit_bytes=VMEM_LIMIT),
        name="stick_breaking",
    )(proj, proj, proj)


POST_MIX_PARTS = 2


def _post_mix_kernel(x_ref, mix_ref, mq_ref, mk_ref, mv_ref, wo_ref, g_ref, o_ref):
    dh = X_HEAD_DIM
    wmix = mix_ref.shape[1]
    mk = mk_ref[...].astype(MXU_DTYPE)
    mv = mv_ref[...].astype(MXU_DTYPE)
    half = x_ref.shape[0] // POST_MIX_PARTS
    for r in range(POST_MIX_PARTS):
        rows = slice(r * half, (r + 1) * half)
        mq = mq_ref[rows, :] * (dh ** -0.5)
        heads = []
        for h in range(X_HEADS):
            cols = slice(h * dh, (h + 1) * dh)
            s = _dot_nt(mq[:, cols], mk[:, cols])
            p = jnp.exp(s - jnp.max(s, axis=-1, keepdims=True))
            inv_l = 1.0 / jnp.sum(p, axis=-1, keepdims=True)
            heads.append(_dot(p, mv[:, cols]) * inv_l)
        cross = jnp.concatenate(heads, axis=-1)
        y = _dot(mix_ref[rows, :], wo_ref[:wmix, :]) + _dot(cross, wo_ref[wmix:, :])
        o_ref[rows, :] = x_ref[rows, :] + _rms(y, g_ref[...])


def _post_mix(x, mix, proj, memq_blk, mem_kv, layer, wo, g, batch, seq, tm):
    t, d = x.shape
    wmix = mix.shape[1]
    xw = X_HEADS * X_HEAD_DIM
    nt = seq // tm
    return pl.pallas_call(
        _post_mix_kernel,
        grid=(batch, nt),
        in_specs=[pl.BlockSpec((tm, d), lambda b, i: (b * nt + i, 0)),
                  pl.BlockSpec((tm, wmix), lambda b, i: (b * nt + i, 0)),
                  pl.BlockSpec((tm, xw), lambda b, i: (b * nt + i, memq_blk)),
                  pl.BlockSpec((N_MEM, xw), lambda b, i: (b, 2 * layer)),
                  pl.BlockSpec((N_MEM, xw), lambda b, i: (b, 2 * layer + 1)),
                  pl.BlockSpec((wmix + xw, d), lambda b, i: (0, 0)),
                  pl.BlockSpec((1, d), lambda b, i: (0, 0))],
        out_specs=pl.BlockSpec((tm, d), lambda b, i: (b * nt + i, 0)),
        out_shape=jax.ShapeDtypeStruct((t, d), F32),
        compiler_params=pltpu.CompilerParams(dimension_semantics=("parallel", "parallel"),
                                             vmem_limit_bytes=VMEM_LIMIT),
        name="post_mix",
    )(x, mix, proj, mem_kv, mem_kv, wo, g.reshape(1, d))


def _mlp_kernel(x_ref, g1_ref, wu_ref, wd_ref, g2_ref, o_ref, *, tf):
    x = x_ref[...]
    h = _rms(x, g1_ref[...]).astype(MXU_DTYPE)
    acc = jnp.zeros(x.shape, F32)
    for c in range(0, wu_ref.shape[1], tf):
        u = jnp.dot(h, wu_ref[:, c:c + tf], preferred_element_type=F32)
        u = jnp.square(jnp.maximum(u, 0.0)).astype(MXU_DTYPE)
        acc = acc + jnp.dot(u, wd_ref[c:c + tf, :], preferred_element_type=F32)
    o_ref[...] = x + _rms(acc, g2_ref[...])


def _mlp(x, g1, wu, wd, g2, tm, tf=1024):
    t, d = x.shape
    f = wu.shape[1]
    return pl.pallas_call(
        functools.partial(_mlp_kernel, tf=tf),
        grid=(t // tm,),
        in_specs=[pl.BlockSpec((tm, d), lambda i: (i, 0)),
                  pl.BlockSpec((1, d), lambda i: (0, 0)),
                  pl.BlockSpec((d, f), lambda i: (0, 0)),
                  pl.BlockSpec((f, d), lambda i: (0, 0)),
                  pl.BlockSpec((1, d), lambda i: (0, 0))],
        out_specs=pl.BlockSpec((tm, d), lambda i: (i, 0)),
        out_shape=jax.ShapeDtypeStruct((t, d), F32),
        compiler_params=pltpu.CompilerParams(dimension_semantics=("parallel",),
                                             vmem_limit_bytes=VMEM_LIMIT),
        name="mlp",
    )(x, g1.reshape(1, d), wu, wd, g2.reshape(1, d))


def kernel(x, mem, mem_norm, norm_pre_mix, norm_post_mix, norm_pre_mlp, norm_post_mlp,
           w_in_a, conv_w_a, a_log_a, dt_bias_a, onorm_a, w_in_b, w_mem_kv, w_out,
           w_up, w_down):
    batch, seq, d = x.shape
    depth = norm_pre_mix.shape[0]
    n_lin = a_log_a.shape[1]
    wseq = n_lin * GDN_HEAD_DIM
    xw = X_HEADS * X_HEAD_DIM
    tm = min(512, seq)
    wt = MXU_DTYPE

    xf = x.reshape(batch * seq, d)
    w_kv = jnp.concatenate([w_mem_kv[i] for i in range(depth)], axis=1).astype(wt)
    mem_kv = _norm_matmul(mem.reshape(batch * N_MEM, d), mem_norm, w_kv, tm=min(512, batch * N_MEM))

    for i in range(depth):
        j = i // 2
        if i % 2 == 0:
            w = w_in_a[j]
            n_main = 4 * wseq
            ba_pad = jnp.zeros((d, LANES - 2 * n_lin), w.dtype)
            w = jnp.concatenate([w[:, :n_main], w[:, n_main + 2 * n_lin:],
                                 w[:, n_main:n_main + 2 * n_lin], ba_pad], axis=1).astype(wt)
            proj = _norm_matmul(xf, norm_pre_mix[i], w, tm=tm)
            mix = _gdn(proj, conv_w_a[j], a_log_a[j], dt_bias_a[j], onorm_a[j], batch, seq, n_lin)
            memq_blk = n_main // xw
        else:
            proj = _norm_matmul(xf, norm_pre_mix[i], w_in_b[j].astype(wt), tm=tm)
            mix = _stick_breaking(proj, batch, seq, wseq)
            memq_blk = 3 * wseq // xw
        xf = _post_mix(xf, mix, proj, memq_blk, mem_kv, i, w_out[i].astype(wt), norm_post_mix[i],
                       batch, seq, tm)
        xf = _mlp(xf, norm_pre_mlp[i], w_up[i].astype(wt), w_down[i].astype(wt), norm_post_mlp[i], tm)
    return xf.reshape(batch, seq, d)
```

```python
import functools

import jax
import jax.numpy as jnp
from jax import lax
from jax.experimental import pallas as pl
from jax.experimental.pallas import tpu as pltpu

EPS = 1e-6
LANES = 128
CONV_WIDTH = 4
GDN_HEAD_DIM = 128
GDN_CHUNK = 64
SB_HEAD_DIM = 64
X_HEADS = 4
X_HEAD_DIM = 64
N_MEM = 256
VMEM_LIMIT = 56 * 1024 * 1024

MXU_DTYPE = jnp.bfloat16
F32 = jnp.float32
LOG2E = 1.4426950408889634


def _dot(a, b):
    return jnp.dot(a.astype(MXU_DTYPE), b.astype(MXU_DTYPE), preferred_element_type=F32)


def _dot_nt(a, b):
    return lax.dot_general(a.astype(MXU_DTYPE), b.astype(MXU_DTYPE),
                           (((1,), (1,)), ((), ())), preferred_element_type=F32)


def _dot_tn(a, b):
    return lax.dot_general(a.astype(MXU_DTYPE), b.astype(MXU_DTYPE),
                           (((0,), (0,)), ((), ())), preferred_element_type=F32)


def _rms(x, g):
    return x * lax.rsqrt(jnp.mean(x * x, axis=-1, keepdims=True) + EPS) * g


def _sigmoid(x):
    return 1.0 / (1.0 + jnp.exp(-x))


def _softplus(x):
    return jnp.maximum(x, 0.0) + jnp.log(1.0 + jnp.exp(-jnp.abs(x)))


MATMUL_N_CHUNK = 512


def _norm_matmul_kernel(x_ref, g_ref, w_ref, o_ref):
    h = _rms(x_ref[...], g_ref[...]).astype(MXU_DTYPE)
    n = w_ref.shape[1]
    for c in range(0, n, MATMUL_N_CHUNK):
        e = min(c + MATMUL_N_CHUNK, n)
        o_ref[:, c:e] = jnp.dot(h, w_ref[:, c:e], preferred_element_type=F32)


def _norm_matmul(x, g, w, tm):
    t, d = x.shape
    n = w.shape[1]
    return pl.pallas_call(
        _norm_matmul_kernel,
        grid=(t // tm,),
        in_specs=[pl.BlockSpec((tm, d), lambda i: (i, 0)),
                  pl.BlockSpec((1, d), lambda i: (0, 0)),
                  pl.BlockSpec((d, n), lambda i: (0, 0))],
        out_specs=pl.BlockSpec((tm, n), lambda i: (i, 0)),
        out_shape=jax.ShapeDtypeStruct((t, n), F32),
        compiler_params=pltpu.CompilerParams(dimension_semantics=("parallel",),
                                             vmem_limit_bytes=VMEM_LIMIT),
        name="norm_matmul",
    )(x, g.reshape(1, d), w)


def _split(x):
    hi = x.astype(MXU_DTYPE)
    lo = (x - hi.astype(F32)).astype(MXU_DTYPE)
    return hi, lo


def _dot_split(ah, al, bh, bl):
    return jnp.dot(jnp.concatenate([ah, ah, al], axis=1), jnp.concatenate([bh, bl, bh], axis=0),
                   preferred_element_type=F32)


GDN_GROUP = 16
GDN_HEADS_PER_STEP = 2


def _gdn_kernel(q_ref, k_ref, v_ref, gate_ref, ba_ref, cwq_ref, cwk_ref, cwv_ref,
                alog_ref, dtb_ref, ogain_ref, o_ref,
                bsig_s, gall_s, qn_s, kn_s, vv_s, beta_s, gc_s, u_s, oo_s, lhs_s, b_s, a_s, cd_s,
                *, n_heads):
    seq = q_ref.shape[0]
    dh = GDN_HEAD_DIM
    ch = GDN_CHUNK
    hps = q_ref.shape[1] // dh
    ba = ba_ref[...]
    bsig_s[...] = _sigmoid(ba)
    gall_s[...] = -jnp.exp(alog_ref[...]) * _softplus(ba + dtb_ref[...])
    for hh in range(hps):
        _gdn_prepare(hh, pl.program_id(1) * hps + hh, q_ref, k_ref, v_ref, cwq_ref, cwk_ref, cwv_ref,
                     bsig_s, gall_s, qn_s, kn_s, vv_s, beta_s, gc_s, u_s, lhs_s, b_s, a_s, cd_s,
                     n_heads=n_heads)

    def chunk_scan(c, states):
        rows = pl.ds(pl.multiple_of(c * ch, ch), ch)
        out = []
        for hh in range(hps):
            r = jnp.dot(lhs_s[hh, c], states[hh].astype(MXU_DTYPE), preferred_element_type=F32)
            v_new = u_s[hh, rows, :] - r[dh:dh + ch]
            oo_s[hh, rows, :] = r[dh + ch:] + _dot(a_s[hh, c], v_new)
            out.append(states[hh] * cd_s[hh, c] + b_s[hh, c] - r[:dh])
        return tuple(out)

    lax.fori_loop(0, seq // ch, chunk_scan, tuple(jnp.zeros((dh, dh), F32) for _ in range(hps)),
                  unroll=8)

    for hh in range(hps):
        cols = slice(hh * dh, (hh + 1) * dh)
        o = oo_s[hh]
        gate = gate_ref[:, cols]
        o = o * lax.rsqrt(jnp.mean(o * o, axis=-1, keepdims=True) + EPS) * ogain_ref[...]
        o_ref[:, cols] = o * (gate * _sigmoid(gate))


def _gdn_prepare(hh, head, q_ref, k_ref, v_ref, cwq_ref, cwk_ref, cwv_ref, bsig_s, gall_s,
                 qn_s, kn_s, vv_s, beta_s, gc_s, u_s, lhs_s, b_s, a_s, cd_s, *, n_heads):
    seq = q_ref.shape[0]
    dh = GDN_HEAD_DIM
    ch = GDN_CHUNK
    blk = 2 * ch
    grp = min(GDN_GROUP, seq // blk)
    ngrp = seq // (blk * grp)
    cols = slice(hh * dh, (hh + 1) * dh)

    row8 = lax.broadcasted_iota(jnp.int32, (8, dh), 0)

    def conv_silu(x_ref, cw_ref):
        cw = cw_ref[:, cols]
        taps = [cw[CONV_WIDTH - 1 - sh:CONV_WIDTH - sh, :] for sh in range(CONV_WIDTH)]
        y = x_ref[8:, cols] * taps[0]
        for sh in range(1, CONV_WIDTH):
            y = y + x_ref[8 - sh:seq - sh, cols] * taps[sh]
        x0 = x_ref[:8, cols]
        y0 = x0 * taps[0]
        for sh in range(1, CONV_WIDTH):
            y0 = y0 + jnp.where(row8 >= sh, pltpu.roll(x0, sh, 0), 0.0) * taps[sh]
        y = jnp.concatenate([y0, y], axis=0)
        return y * _sigmoid(y)

    def l2n(x):
        return x * lax.rsqrt(jnp.sum(x * x, axis=-1, keepdims=True) + EPS)

    qn_s[...] = l2n(conv_silu(q_ref, cwq_ref)) * (dh ** -0.5)
    kn_s[...] = l2n(conv_silu(k_ref, cwk_ref))
    vv_s[...] = conv_silu(v_ref, cwv_ref)

    lane = lax.broadcasted_iota(jnp.int32, (seq, LANES), 1)
    beta_col = jnp.sum(jnp.where(lane == head, bsig_s[...], 0.0), axis=-1, keepdims=True)
    g_col = jnp.sum(jnp.where(lane == head + n_heads, gall_s[...], 0.0), axis=-1, keepdims=True)
    beta_s[...] = jnp.broadcast_to(beta_col, (seq, dh))
    g_b = jnp.broadcast_to(g_col, (seq, dh))
    rb = lax.broadcasted_iota(jnp.int32, (blk, blk), 0)
    cb = lax.broadcasted_iota(jnp.int32, (blk, blk), 1)
    tri = jnp.where(jnp.logical_and(rb >= cb, (rb // ch) == (cb // ch)), 1.0, 0.0).astype(MXU_DTYPE)
    tri3 = jnp.concatenate([tri, tri, tri], axis=1)
    for j in range(seq // blk):
        x = g_b[j * blk:(j + 1) * blk]
        hi = x.astype(MXU_DTYPE)
        r1 = x - hi.astype(F32)
        mid = r1.astype(MXU_DTYPE)
        lo = (r1 - mid.astype(F32)).astype(MXU_DTYPE)
        gc_s[j * blk:(j + 1) * blk, :] = jnp.dot(tri3, jnp.concatenate([hi, mid, lo], axis=0),
                                                 preferred_element_type=F32)

    ri = lax.broadcasted_iota(jnp.int32, (ch, blk), 0)
    ci = lax.broadcasted_iota(jnp.int32, (ch, blk), 1)
    lane_lo = ci < ch
    causal = ri >= (ci & (ch - 1))
    strict = ri > (ci & (ch - 1))
    eye = jnp.where(ri == (ci & (ch - 1)), 1.0, 0.0).astype(F32)
    first = lax.broadcasted_iota(jnp.int32, (blk, blk), 0) < ch

    def side_by_side(x):
        return jnp.where(lane_lo, x[:ch], x[ch:])

    def block_diag(x):
        zero = jnp.zeros_like(x)
        return jnp.concatenate([jnp.where(lane_lo, x, zero), jnp.where(lane_lo, zero, x)], axis=0)

    def group_prep(jg, _):
        idx = [jg * grp + g for g in range(grp)]
        sls = [pl.ds(pl.multiple_of(i * blk, blk), blk) for i in idx]
        q = [qn_s[s, :] for s in sls]
        k = [kn_s[s, :] for s in sls]
        v = [vv_s[s, :] for s in sls]
        bb = [beta_s[s, :] for s in sls]
        gcb = [gc_s[s, :] for s in sls]
        gcr = [jnp.where(lane_lo, x[:ch], x[ch:]) for x in gcb]
        decay = [jnp.exp(jnp.where(causal, a - x.T[:ch], -1e30)) for a, x in zip(gcr, gcb)]
        egc = [jnp.exp(x) for x in gcb]
        kb = [a * b for a, b in zip(k, bb)]
        kq = [_dot_nt(jnp.concatenate([a, b], axis=0), c) for a, b, c in zip(kb, q, k)]
        low = [jnp.where(strict, side_by_side(a[:blk]) * d, 0.0) for a, d in zip(kq, decay)]
        intra = [side_by_side(a[blk:]) * d for a, d in zip(kq, decay)]
        inv = [eye - x for x in low]
        pw = [_split(x) for x in low]
        pw = [_dot_split(h, l, block_diag(h), block_diag(l)) for h, l in pw]
        n = 2
        while n < ch:
            psp = [_split(x) for x in pw]
            isp = [_split(x) for x in inv]
            last = 2 * n >= ch
            prod = [_dot_split(ih if last else jnp.concatenate([ih, h], axis=0),
                               il if last else jnp.concatenate([il, l], axis=0),
                               block_diag(h), block_diag(l))
                    for (ih, il), (h, l) in zip(isp, psp)]
            inv = [x + y[:ch] for x, y in zip(inv, prod)]
            if not last:
                pw = [y[ch:] for y in prod]
            n *= 2
        rhs = [_split(jnp.concatenate([a * b, c * e], axis=-1)) for a, b, c, e in zip(v, bb, kb, egc)]
        isp = [_split(x) for x in inv]
        sol = [_dot_split(block_diag(ih), block_diag(il), rh, rl) for (ih, il), (rh, rl) in zip(isp, rhs)]
        for g in range(grp):
            gl = [gcb[g][ch - 1:ch, :], gcb[g][blk - 1:blk, :]]
            kd = k[g] * jnp.exp(jnp.where(first, gl[0], gl[1]) - gcb[g])
            qd = q[g] * egc[g]
            u_s[hh, sls[g], :] = sol[g][:, :dh]
            for half in range(2):
                rs = slice(half * ch, (half + 1) * ch)
                c = 2 * idx[g] + half
                mb = _dot_tn(kd[rs], sol[g][rs])
                lhs_s[hh, c] = jnp.concatenate([mb[:, dh:], sol[g][rs, dh:], qd[rs]], axis=0).astype(MXU_DTYPE)
                b_s[hh, c] = mb[:, :dh]
                a_s[hh, c] = intra[g][:, rs].astype(MXU_DTYPE)
                cd_s[hh, c] = jnp.exp(gl[half])
        return 0

    lax.fori_loop(0, ngrp, group_prep, 0)


def _gdn(proj, conv_w, a_log, dt_bias, o_gain, batch, seq, n_heads):
    dh = GDN_HEAD_DIM
    width = n_heads * dh
    ba_blk = proj.shape[1] // LANES - 1
    pad = LANES - 2 * n_heads
    alog_row = jnp.concatenate([jnp.zeros((n_heads,), F32), a_log, jnp.zeros((pad,), F32)]).reshape(1, LANES)
    dtb_row = jnp.concatenate([jnp.zeros((n_heads,), F32), dt_bias, jnp.zeros((pad,), F32)]).reshape(1, LANES)
    hps = GDN_HEADS_PER_STEP
    nblk = n_heads // hps
    col = lambda off: pl.BlockSpec((seq, hps * dh), lambda b, h: (b, h + off))
    cw = lambda off: pl.BlockSpec((CONV_WIDTH, hps * dh), lambda b, h: (0, h + off))
    row = pl.BlockSpec((1, LANES), lambda b, h: (0, 0))
    big = pltpu.VMEM((seq, dh), F32)
    per_head = pltpu.VMEM((hps, seq, dh), F32)
    nc = seq // GDN_CHUNK
    return pl.pallas_call(
        functools.partial(_gdn_kernel, n_heads=n_heads),
        grid=(batch, nblk),
        in_specs=[col(0), col(nblk), col(2 * nblk), col(3 * nblk),
                  pl.BlockSpec((seq, LANES), lambda b, h: (b, ba_blk)),
                  cw(0), cw(nblk), cw(2 * nblk), row, row, row],
        out_specs=pl.BlockSpec((seq, hps * dh), lambda b, h: (b, h)),
        out_shape=jax.ShapeDtypeStruct((batch * seq, width), F32),
        scratch_shapes=[big, big, big, big, big, big, big, per_head, per_head,
                        pltpu.VMEM((hps, nc, 2 * dh, dh), MXU_DTYPE),
                        pltpu.VMEM((hps, nc, dh, dh), F32),
                        pltpu.VMEM((hps, nc, GDN_CHUNK, GDN_CHUNK), MXU_DTYPE),
                        pltpu.VMEM((hps, nc, 1, dh), F32)],
        compiler_params=pltpu.CompilerParams(dimension_semantics=("parallel", "parallel"),
                                             vmem_limit_bytes=VMEM_LIMIT),
        name="gated_deltanet",
    )(proj, proj, proj, proj, proj, conv_w, conv_w, conv_w, alog_row, dtb_row,
      o_gain.reshape(1, dh))


SB_KEY_TILE = 256
SB_SUBTILES = 2


def _sb_kernel(q_ref, k_ref, v_ref, o_ref, pre_s, rs_s, acc_s, carry_s):
    qi = pl.program_id(2)
    tq = q_ref.shape[0]
    tk = SB_KEY_TILE
    nsub = SB_SUBTILES
    nh = LANES // SB_HEAD_DIM
    lane_q = lax.broadcasted_iota(jnp.int32, (tq, LANES), 1)
    after = jnp.where(lax.broadcasted_iota(jnp.int32, (tk, tk), 0) >= lax.broadcasted_iota(jnp.int32, (tk, tk), 1),
                      1.0, 0.0).astype(MXU_DTYPE)

    q_all = q_ref[...] * (SB_HEAD_DIM ** -0.5)
    qs = [jnp.where((lane_q // SB_HEAD_DIM) == h, q_all, 0.0).astype(MXU_DTYPE) for h in range(nh)]

    def score_stage(kstart, slot, diag):
        for h in range(nh):
            for s in range(nsub):
                ks = pl.ds(pl.multiple_of(kstart + s * tk, tk), tk)
                r0 = s * tk if diag else 0
                z = _dot_nt(qs[h][r0:], k_ref[ks, :])
                p = jnp.maximum(z, 0.0) + jnp.log(1.0 + jnp.exp2(jnp.abs(z) * (-LOG2E)))
                if diag:
                    shape = (tq - r0, tk)
                    before = lax.broadcasted_iota(jnp.int32, shape, 1) < lax.broadcasted_iota(jnp.int32, shape, 0)
                    p = jnp.where(before, p, 0.0)
                t = jnp.dot(p.astype(MXU_DTYPE), after, preferred_element_type=F32)
                pre = z - t
                if diag:
                    pre = jnp.where(before, pre, -1e30)
                pre_s[slot, h, r0:, s * tk:(s + 1) * tk] = pre
                rs_s[slot, h * nsub + s, r0:] = t[:, 0:1]
                if r0:
                    pre_s[slot, h, :r0, s * tk:(s + 1) * tk] = jnp.full((r0, tk), -1e30, F32)
                    rs_s[slot, h * nsub + s, :r0] = jnp.zeros((r0, 1), F32)

    def value_stage(kstart, slot):
        for h in range(nh):
            carry = carry_s[h]
            acc = jnp.zeros((tq, LANES), F32)
            for s in reversed(range(nsub)):
                ks = pl.ds(pl.multiple_of(kstart + s * tk, tk), tk)
                a = jnp.exp(pre_s[slot, h, :, s * tk:(s + 1) * tk] - carry)
                acc = acc + _dot(a, v_ref[ks, :])
                carry = carry + rs_s[slot, h * nsub + s]
            carry_s[h] = carry
            acc_s[h] = acc_s[h] + acc

    acc_s[...] = jnp.zeros(acc_s.shape, F32)
    carry_s[...] = jnp.zeros(carry_s.shape, F32)
    score_stage(qi * tq, 0, True)

    def body(i, _):
        value_stage((qi - i) * tq, i & 1)
        score_stage((qi - 1 - i) * tq, (i + 1) & 1, False)
        return 0

    lax.fori_loop(0, qi, body, 0)
    value_stage(0, qi & 1)
    out = acc_s[0]
    for h in range(1, nh):
        out = jnp.where((lane_q // SB_HEAD_DIM) == h, acc_s[h], out)
    o_ref[...] = out


def _stick_breaking(proj, batch, seq, width):
    tq = SB_KEY_TILE * SB_SUBTILES
    npair = width // LANES
    nq = seq // tq
    nh = LANES // SB_HEAD_DIM
    return pl.pallas_call(
        _sb_kernel,
        grid=(batch, npair, nq),
        in_specs=[pl.BlockSpec((tq, LANES), lambda b, p, i: (b * nq + i, p)),
                  pl.BlockSpec((seq, LANES), lambda b, p, i: (b, npair + p)),
                  pl.BlockSpec((seq, LANES), lambda b, p, i: (b, 2 * npair + p))],
        out_specs=pl.BlockSpec((tq, LANES), lambda b, p, i: (b * nq + i, p)),
        out_shape=jax.ShapeDtypeStruct((batch * seq, width), F32),
        scratch_shapes=[pltpu.VMEM((2, nh, tq, tq), F32),
                        pltpu.VMEM((2, nh * SB_SUBTILES, tq, 1), F32),
                        pltpu.VMEM((nh, tq, LANES), F32),
                        pltpu.VMEM((nh, tq, 1), F32)],
        compiler_params=pltpu.CompilerParams(
            dimension_semantics=("parallel", "parallel", "arbitrary"),
            vmem_limit_bytes=VMEM_LIMIT),
        name="stick_breaking",
    )(proj, proj, proj)


POST_MIX_PARTS = 2


def _post_mix_kernel(x_ref, mix_ref, mq_ref, mk_ref, mv_ref, wo_ref, g_ref, o_ref):
    dh = X_HEAD_DIM
    wmix = mix_ref.shape[1]
    mk = mk_ref[...].astype(MXU_DTYPE)
    mv = mv_ref[...].astype(MXU_DTYPE)
    half = x_ref.shape[0] // POST_MIX_PARTS
    for r in range(POST_MIX_PARTS):
        rows = slice(r * half, (r + 1) * half)
        mq = mq_ref[rows, :] * (dh ** -0.5)
        heads = []
        for h in range(X_HEADS):
            cols = slice(h * dh, (h + 1) * dh)
            s = _dot_nt(mq[:, cols], mk[:, cols])
            p = jnp.exp(s - jnp.max(s, axis=-1, keepdims=True))
            inv_l = 1.0 / jnp.sum(p, axis=-1, keepdims=True)
            heads.append(_dot(p, mv[:, cols]) * inv_l)
        cross = jnp.concatenate(heads, axis=-1)
        y = _dot(mix_ref[rows, :], wo_ref[:wmix, :]) + _dot(cross, wo_ref[wmix:, :])
        o_ref[rows, :] = x_ref[rows, :] + _rms(y, g_ref[...])


def _post_mix(x, mix, proj, memq_blk, mem_kv, layer, wo, g, batch, seq, tm):
    t, d = x.shape
    wmix = mix.shape[1]
    xw = X_HEADS * X_HEAD_DIM
    nt = seq // tm
    return pl.pallas_call(
        _post_mix_kernel,
        grid=(batch, nt),
        in_specs=[pl.BlockSpec((tm, d), lambda b, i: (b * nt + i, 0)),
                  pl.BlockSpec((tm, wmix), lambda b, i: (b * nt + i, 0)),
                  pl.BlockSpec((tm, xw), lambda b, i: (b * nt + i, memq_blk)),
                  pl.BlockSpec((N_MEM, xw), lambda b, i: (b, 2 * layer)),
                  pl.BlockSpec((N_MEM, xw), lambda b, i: (b, 2 * layer + 1)),
                  pl.BlockSpec((wmix + xw, d), lambda b, i: (0, 0)),
                  pl.BlockSpec((1, d), lambda b, i: (0, 0))],
        out_specs=pl.BlockSpec((tm, d), lambda b, i: (b * nt + i, 0)),
        out_shape=jax.ShapeDtypeStruct((t, d), F32),
        compiler_params=pltpu.CompilerParams(dimension_semantics=("parallel", "parallel"),
                                             vmem_limit_bytes=VMEM_LIMIT),
        name="post_mix",
    )(x, mix, proj, mem_kv, mem_kv, wo, g.reshape(1, d))


def _mlp_kernel(x_ref, g1_ref, wu_ref, wd_ref, g2_ref, o_ref, *, tf):
    x = x_ref[...]
    h = _rms(x, g1_ref[...]).astype(MXU_DTYPE)
    acc = jnp.zeros(x.shape, F32)
    for c in range(0, wu_ref.shape[1], tf):
        u = jnp.dot(h, wu_ref[:, c:c + tf], preferred_element_type=F32)
        u = jnp.square(jnp.maximum(u, 0.0)).astype(MXU_DTYPE)
        acc = acc + jnp.dot(u, wd_ref[c:c + tf, :], preferred_element_type=F32)
    o_ref[...] = x + _rms(acc, g2_ref[...])


def _mlp(x, g1, wu, wd, g2, tm, tf=1024):
    t, d = x.shape
    f = wu.shape[1]
    return pl.pallas_call(
        functools.partial(_mlp_kernel, tf=tf),
        grid=(t // tm,),
        in_specs=[pl.BlockSpec((tm, d), lambda i: (i, 0)),
                  pl.BlockSpec((1, d), lambda i: (0, 0)),
                  pl.BlockSpec((d, f), lambda i: (0, 0)),
                  pl.BlockSpec((f, d), lambda i: (0, 0)),
                  pl.BlockSpec((1, d), lambda i: (0, 0))],
        out_specs=pl.BlockSpec((tm, d), lambda i: (i, 0)),
        out_shape=jax.ShapeDtypeStruct((t, d), F32),
        compiler_params=pltpu.CompilerParams(dimension_semantics=("parallel",),
                                             vmem_limit_bytes=VMEM_LIMIT),
        name="mlp",
    )(x, g1.reshape(1, d), wu, wd, g2.reshape(1, d))


def kernel(x, mem, mem_norm, norm_pre_mix, norm_post_mix, norm_pre_mlp, norm_post_mlp,
           w_in_a, conv_w_a, a_log_a, dt_bias_a, onorm_a, w_in_b, w_mem_kv, w_out,
           w_up, w_down):
    batch, seq, d = x.shape
    depth = norm_pre_mix.shape[0]
    n_lin = a_log_a.shape[1]
    wseq = n_lin * GDN_HEAD_DIM
    xw = X_HEADS * X_HEAD_DIM
    tm = min(512, seq)
    wt = MXU_DTYPE

    xf = x.reshape(batch * seq, d)
    w_kv = jnp.concatenate([w_mem_kv[i] for i in range(depth)], axis=1).astype(wt)
    mem_kv = _norm_matmul(mem.reshape(batch * N_MEM, d), mem_norm, w_kv, tm=min(512, batch * N_MEM))

    for i in range(depth):
        j = i // 2
        if i % 2 == 0:
            w = w_in_a[j]
            n_main = 4 * wseq
            ba_pad = jnp.zeros((d, LANES - 2 * n_lin), w.dtype)
            w = jnp.concatenate([w[:, :n_main], w[:, n_main + 2 * n_lin:],
                                 w[:, n_main:n_main + 2 * n_lin], ba_pad], axis=1).astype(wt)
            proj = _norm_matmul(xf, norm_pre_mix[i], w, tm=tm)
            mix = _gdn(proj, conv_w_a[j], a_log_a[j], dt_bias_a[j], onorm_a[j], batch, seq, n_lin)
            memq_blk = n_main // xw
        else:
            proj = _norm_matmul(xf, norm_pre_mix[i], w_in_b[j].astype(wt), tm=tm)
            mix = _stick_breaking(proj, batch, seq, wseq)
            memq_blk = 3 * wseq // xw
        xf = _post_mix(xf, mix, proj, memq_blk, mem_kv, i, w_out[i].astype(wt), norm_post_mix[i],
                       batch, seq, tm)
        xf = _mlp(xf, norm_pre_mlp[i], w_up[i].astype(wt), w_down[i].astype(wt), norm_post_mlp[i], tm)
    return xf.reshape(batch, seq, d)
```

```python
import functools

import jax
import jax.numpy as jnp
from jax import lax
from jax.experimental import pallas as pl
from jax.experimental.pallas import tpu as pltpu

EPS = 1e-6
LANES = 128
CONV_WIDTH = 4
GDN_HEAD_DIM = 128
GDN_CHUNK = 64
SB_HEAD_DIM = 64
X_HEADS = 4
X_HEAD_DIM = 64
N_MEM = 256
VMEM_LIMIT = 56 * 1024 * 1024

MXU_DTYPE = jnp.bfloat16
F32 = jnp.float32
LOG2E = 1.4426950408889634


def _dot(a, b):
    return jnp.dot(a.astype(MXU_DTYPE), b.astype(MXU_DTYPE), preferred_element_type=F32)


def _dot_nt(a, b):
    return lax.dot_general(a.astype(MXU_DTYPE), b.astype(MXU_DTYPE),
                           (((1,), (1,)), ((), ())), preferred_element_type=F32)


def _dot_tn(a, b):
    return lax.dot_general(a.astype(MXU_DTYPE), b.astype(MXU_DTYPE),
                           (((0,), (0,)), ((), ())), preferred_element_type=F32)


def _rms(x, g):
    return x * lax.rsqrt(jnp.mean(x * x, axis=-1, keepdims=True) + EPS) * g


def _sigmoid(x):
    return 1.0 / (1.0 + jnp.exp(-x))


def _softplus(x):
    return jnp.maximum(x, 0.0) + jnp.log(1.0 + jnp.exp(-jnp.abs(x)))


MATMUL_N_CHUNK = 512


def _norm_matmul_kernel(x_ref, g_ref, w_ref, o_ref):
    h = _rms(x_ref[...], g_ref[...]).astype(MXU_DTYPE)
    n = w_ref.shape[1]
    for c in range(0, n, MATMUL_N_CHUNK):
        e = min(c + MATMUL_N_CHUNK, n)
        o_ref[:, c:e] = jnp.dot(h, w_ref[:, c:e], preferred_element_type=F32)


def _norm_matmul(x, g, w, tm):
    t, d = x.shape
    n = w.shape[1]
    return pl.pallas_call(
        _norm_matmul_kernel,
        grid=(t // tm,),
        in_specs=[pl.BlockSpec((tm, d), lambda i: (i, 0)),
                  pl.BlockSpec((1, d), lambda i: (0, 0)),
                  pl.BlockSpec((d, n), lambda i: (0, 0))],
        out_specs=pl.BlockSpec((tm, n), lambda i: (i, 0)),
        out_shape=jax.ShapeDtypeStruct((t, n), F32),
        compiler_params=pltpu.CompilerParams(dimension_semantics=("parallel",),
                                             vmem_limit_bytes=VMEM_LIMIT),
        name="norm_matmul",
    )(x, g.reshape(1, d), w)


def _split(x):
    hi = x.astype(MXU_DTYPE)
    lo = (x - hi.astype(F32)).astype(MXU_DTYPE)
    return hi, lo


def _dot_split(ah, al, bh, bl):
    return jnp.dot(jnp.concatenate([ah, ah, al], axis=1), jnp.concatenate([bh, bl, bh], axis=0),
                   preferred_element_type=F32)


GDN_GROUP = 16
GDN_HEADS_PER_STEP = 2


def _gdn_kernel(q_ref, k_ref, v_ref, gate_ref, ba_ref, cwq_ref, cwk_ref, cwv_ref,
                alog_ref, dtb_ref, ogain_ref, o_ref,
                bsig_s, gall_s, qn_s, kn_s, vv_s, beta_s, gc_s, u_s, oo_s, lhs_s, b_s, a_s, cd_s,
                *, n_heads):
    seq = q_ref.shape[0]
    dh = GDN_HEAD_DIM
    ch = GDN_CHUNK
    hps = q_ref.shape[1] // dh
    ba = ba_ref[...]
    bsig_s[...] = _sigmoid(ba)
    gall_s[...] = -jnp.exp(alog_ref[...]) * _softplus(ba + dtb_ref[...])
    for hh in range(hps):
        _gdn_prepare(hh, pl.program_id(1) * hps + hh, q_ref, k_ref, v_ref, cwq_ref, cwk_ref, cwv_ref,
                     bsig_s, gall_s, qn_s, kn_s, vv_s, beta_s, gc_s, u_s, lhs_s, b_s, a_s, cd_s,
                     n_heads=n_heads)

    def chunk_scan(c, states):
        rows = pl.ds(pl.multiple_of(c * ch, ch), ch)
        out = []
        for hh in range(hps):
            r = jnp.dot(lhs_s[hh, c], states[hh].astype(MXU_DTYPE), preferred_element_type=F32)
            v_new = u_s[hh, rows, :] - r[dh:dh + ch]
            oo_s[hh, rows, :] = r[dh + ch:] + _dot(a_s[hh, c], v_new)
            out.append(states[hh] * cd_s[hh, c] + b_s[hh, c] - r[:dh])
        return tuple(out)

    lax.fori_loop(0, seq // ch, chunk_scan, tuple(jnp.zeros((dh, dh), F32) for _ in range(hps)),
                  unroll=8)

    for hh in range(hps):
        cols = slice(hh * dh, (hh + 1) * dh)
        o = oo_s[hh]
        gate = gate_ref[:, cols]
        o = o * lax.rsqrt(jnp.mean(o * o, axis=-1, keepdims=True) + EPS) * ogain_ref[...]
        o_ref[:, cols] = o * (gate * _sigmoid(gate))


def _gdn_prepare(hh, head, q_ref, k_ref, v_ref, cwq_ref, cwk_ref, cwv_ref, bsig_s, gall_s,
                 qn_s, kn_s, vv_s, beta_s, gc_s, u_s, lhs_s, b_s, a_s, cd_s, *, n_heads):
    seq = q_ref.shape[0]
    dh = GDN_HEAD_DIM
    ch = GDN_CHUNK
    blk = 2 * ch
    grp = min(GDN_GROUP, seq // blk)
    ngrp = seq // (blk * grp)
    cols = slice(hh * dh, (hh + 1) * dh)

    row8 = lax.broadcasted_iota(jnp.int32, (8, dh), 0)

    def conv_silu(x_ref, cw_ref):
        cw = cw_ref[:, cols]
        taps = [cw[CONV_WIDTH - 1 - sh:CONV_WIDTH - sh, :] for sh in range(CONV_WIDTH)]
        y = x_ref[8:, cols] * taps[0]
        for sh in range(1, CONV_WIDTH):
            y = y + x_ref[8 - sh:seq - sh, cols] * taps[sh]
        x0 = x_ref[:8, cols]
        y0 = x0 * taps[0]
        for sh in range(1, CONV_WIDTH):
            y0 = y0 + jnp.where(row8 >= sh, pltpu.roll(x0, sh, 0), 0.0) * taps[sh]
        y = jnp.concatenate([y0, y], axis=0)
        return y * _sigmoid(y)

    def l2n(x):
        return x * lax.rsqrt(jnp.sum(x * x, axis=-1, keepdims=True) + EPS)

    qn_s[...] = l2n(conv_silu(q_ref, cwq_ref)) * (dh ** -0.5)
    kn_s[...] = l2n(conv_silu(k_ref, cwk_ref))
    vv_s[...] = conv_silu(v_ref, cwv_ref)

    lane = lax.broadcasted_iota(jnp.int32, (seq, LANES), 1)
    beta_col = jnp.sum(jnp.where(lane == head, bsig_s[...], 0.0), axis=-1, keepdims=True)
    g_col = jnp.sum(jnp.where(lane == head + n_heads, gall_s[...], 0.0), axis=-1, keepdims=True)
    beta_s[...] = jnp.broadcast_to(beta_col, (seq, dh))
    g_b = jnp.broadcast_to(g_col, (seq, dh))
    rb = lax.broadcasted_iota(jnp.int32, (blk, blk), 0)
    cb = lax.broadcasted_iota(jnp.int32, (blk, blk), 1)
    tri = jnp.where(jnp.logical_and(rb >= cb, (rb // ch) == (cb // ch)), 1.0, 0.0).astype(MXU_DTYPE)
    tri3 = jnp.concatenate([tri, tri, tri], axis=1)
    for j in range(seq // blk):
        x = g_b[j * blk:(j + 1) * blk]
        hi = x.astype(MXU_DTYPE)
        r1 = x - hi.astype(F32)
        mid = r1.astype(MXU_DTYPE)
        lo = (r1 - mid.astype(F32)).astype(MXU_DTYPE)
        gc_s[j * blk:(j + 1) * blk, :] = jnp.dot(tri3, jnp.concatenate([hi, mid, lo], axis=0),
                                                 preferred_element_type=F32)

    ri = lax.broadcasted_iota(jnp.int32, (ch, blk), 0)
    ci = lax.broadcasted_iota(jnp.int32, (ch, blk), 1)
    lane_lo = ci < ch
    causal = ri >= (ci & (ch - 1))
    strict = ri > (ci & (ch - 1))
    eye = jnp.where(ri == (ci & (ch - 1)), 1.0, 0.0).astype(F32)
    first = lax.broadcasted_iota(jnp.int32, (blk, blk), 0) < ch

    def side_by_side(x):
        return jnp.where(lane_lo, x[:ch], x[ch:])

    def block_diag(x):
        zero = jnp.zeros_like(x)
        return jnp.concatenate([jnp.where(lane_lo, x, zero), jnp.where(lane_lo, zero, x)], axis=0)

    def group_prep(jg, _):
        idx = [jg * grp + g for g in range(grp)]
        sls = [pl.ds(pl.multiple_of(i * blk, blk), blk) for i in idx]
        q = [qn_s[s, :] for s in sls]
        k = [kn_s[s, :] for s in sls]
        v = [vv_s[s, :] for s in sls]
        bb = [beta_s[s, :] for s in sls]
        gcb = [gc_s[s, :] for s in sls]
        gcr = [jnp.where(lane_lo, x[:ch], x[ch:]) for x in gcb]
        decay = [jnp.exp(jnp.where(causal, a - x.T[:ch], -1e30)) for a, x in zip(gcr, gcb)]
        egc = [jnp.exp(x) for x in gcb]
        kb = [a * b for a, b in zip(k, bb)]
        kq = [_dot_nt(jnp.concatenate([a, b], axis=0), c) for a, b, c in zip(kb, q, k)]
        low = [jnp.where(strict, side_by_side(a[:blk]) * d, 0.0) for a, d in zip(kq, decay)]
        intra = [side_by_side(a[blk:]) * d for a, d in zip(kq, decay)]
        inv = [eye - x for x in low]
        pw = [_split(x) for x in low]
        pw = [_dot_split(h, l, block_diag(h), block_diag(l)) for h, l in pw]
        n = 2
        while n < ch:
            psp = [_split(x) for x in pw]
            isp = [_split(x) for x in inv]
            last = 2 * n >= ch
            prod = [_dot_split(ih if last else jnp.concatenate([ih, h], axis=0),
                               il if last else jnp.concatenate([il, l], axis=0),
                               block_diag(h), block_diag(l))
                    for (ih, il), (h, l) in zip(isp, psp)]
            inv = [x + y[:ch] for x, y in zip(inv, prod)]
            if not last:
                pw = [y[ch:] for y in prod]
            n *= 2
        rhs = [_split(jnp.concatenate([a * b, c * e], axis=-1)) for a, b, c, e in zip(v, bb, kb, egc)]
        isp = [_split(x) for x in inv]
        sol = [_dot_split(block_diag(ih), block_diag(il), rh, rl) for (ih, il), (rh, rl) in zip(isp, rhs)]
        for g in range(grp):
            gl = [gcb[g][ch - 1:ch, :], gcb[g][blk - 1:blk, :]]
            kd = k[g] * jnp.exp(jnp.where(first, gl[0], gl[1]) - gcb[g])
            qd = q[g] * egc[g]
            u_s[hh, sls[g], :] = sol[g][:, :dh]
            for half in range(2):
                rs = slice(half * ch, (half + 1) * ch)
                c = 2 * idx[g] + half
                mb = _dot_tn(kd[rs], sol[g][rs])
                lhs_s[hh, c] = jnp.concatenate([mb[:, dh:], sol[g][rs, dh:], qd[rs]], axis=0).astype(MXU_DTYPE)
                b_s[hh, c] = mb[:, :dh]
                a_s[hh, c] = intra[g][:, rs].astype(MXU_DTYPE)
                cd_s[hh, c] = jnp.exp(gl[half])
        return 0

    lax.fori_loop(0, ngrp, group_prep, 0)


def _gdn(proj, conv_w, a_log, dt_bias, o_gain, batch, seq, n_heads):
    dh = GDN_HEAD_DIM
    width = n_heads * dh
    ba_blk = proj.shape[1] // LANES - 1
    pad = LANES - 2 * n_heads
    alog_row = jnp.concatenate([jnp.zeros((n_heads,), F32), a_log, jnp.zeros((pad,), F32)]).reshape(1, LANES)
    dtb_row = jnp.concatenate([jnp.zeros((n_heads,), F32), dt_bias, jnp.zeros((pad,), F32)]).reshape(1, LANES)
    hps = GDN_HEADS_PER_STEP
    nblk = n_heads // hps
    col = lambda off: pl.BlockSpec((seq, hps * dh), lambda b, h: (b, h + off))
    cw = lambda off: pl.BlockSpec((CONV_WIDTH, hps * dh), lambda b, h: (0, h + off))
    row = pl.BlockSpec((1, LANES), lambda b, h: (0, 0))
    big = pltpu.VMEM((seq, dh), F32)
    per_head = pltpu.VMEM((hps, seq, dh), F32)
    nc = seq // GDN_CHUNK
    return pl.pallas_call(
        functools.partial(_gdn_kernel, n_heads=n_heads),
        grid=(batch, nblk),
        in_specs=[col(0), col(nblk), col(2 * nblk), col(3 * nblk),
                  pl.BlockSpec((seq, LANES), lambda b, h: (b, ba_blk)),
                  cw(0), cw(nblk), cw(2 * nblk), row, row, row],
        out_specs=pl.BlockSpec((seq, hps * dh), lambda b, h: (b, h)),
        out_shape=jax.ShapeDtypeStruct((batch * seq, width), F32),
        scratch_shapes=[big, big, big, big, big, big, big, per_head, per_head,
                        pltpu.VMEM((hps, nc, 2 * dh, dh), MXU_DTYPE),
                        pltpu.VMEM((hps, nc, dh, dh), F32),
                        pltpu.VMEM((hps, nc, GDN_CHUNK, GDN_CHUNK), MXU_DTYPE),
                        pltpu.VMEM((hps, nc, 1, dh), F32)],
        compiler_params=pltpu.CompilerParams(dimension_semantics=("parallel", "parallel"),
                                             vmem_limit_bytes=VMEM_LIMIT),
        name="gated_deltanet",
    )(proj, proj, proj, proj, proj, conv_w, conv_w, conv_w, alog_row, dtb_row,
      o_gain.reshape(1, dh))


SB_KEY_TILE = 256
SB_SUBTILES = 2
SB_DEAD = 104.0


def _sb_kernel(q_ref, k_ref, v_ref, o_ref, pre_s, rs_s, acc_s, carry_s):
    qi = pl.program_id(2)
    tq = q_ref.shape[0]
    tk = SB_KEY_TILE
    nsub = SB_SUBTILES
    nh = LANES // SB_HEAD_DIM
    lane_q = lax.broadcasted_iota(jnp.int32, (tq, LANES), 1)
    after = jnp.where(lax.broadcasted_iota(jnp.int32, (tk, tk), 0) >= lax.broadcasted_iota(jnp.int32, (tk, tk), 1),
                      1.0, 0.0).astype(MXU_DTYPE)

    q_all = q_ref[...] * (SB_HEAD_DIM ** -0.5)
    qs = [jnp.where((lane_q // SB_HEAD_DIM) == h, q_all, 0.0).astype(MXU_DTYPE) for h in range(nh)]

    def score_stage(kstart, slot, diag):
        for h in range(nh):
            for s in range(nsub):
                ks = pl.ds(pl.multiple_of(kstart + s * tk, tk), tk)
                r0 = s * tk if diag else 0
                z = _dot_nt(qs[h][r0:], k_ref[ks, :])
                p = jnp.maximum(z, 0.0) + jnp.log(1.0 + jnp.exp2(jnp.abs(z) * (-LOG2E)))
                if diag:
                    shape = (tq - r0, tk)
                    before = lax.broadcasted_iota(jnp.int32, shape, 1) < lax.broadcasted_iota(jnp.int32, shape, 0)
                    p = jnp.where(before, p, 0.0)
                t = jnp.dot(p.astype(MXU_DTYPE), after, preferred_element_type=F32)
                pre = jnp.minimum(z - t, 0.0)
                if diag:
                    pre = jnp.where(before, pre, -1e30)
                pre_s[slot, h, r0:, s * tk:(s + 1) * tk] = pre
                rs_s[slot, h * nsub + s, r0:] = t[:, 0:1]
                if r0:
                    pre_s[slot, h, :r0, s * tk:(s + 1) * tk] = jnp.full((r0, tk), -1e30, F32)
                    rs_s[slot, h * nsub + s, :r0] = jnp.zeros((r0, 1), F32)

    def value_stage(kstart, slot):
        for h in range(nh):
            carry = carry_s[h]
            acc = jnp.zeros((tq, LANES), F32)
            for s in reversed(range(nsub)):
                ks = pl.ds(pl.multiple_of(kstart + s * tk, tk), tk)
                a = jnp.exp(pre_s[slot, h, :, s * tk:(s + 1) * tk] - carry)
                acc = acc + _dot(a, v_ref[ks, :])
                carry = carry + rs_s[slot, h * nsub + s]
            carry_s[h] = carry
            acc_s[h] = acc_s[h] + acc

    acc_s[...] = jnp.zeros(acc_s.shape, F32)
    carry_s[...] = jnp.zeros(carry_s.shape, F32)
    score_stage(qi * tq, 0, True)

    def live(i, low):
        return jnp.logical_and(i < qi, low < SB_DEAD)

    def body(st):
        i, _ = st
        value_stage((qi - i) * tq, i & 1)
        score_stage((qi - 1 - i) * tq, (i + 1) & 1, False)
        return i + 1, jnp.min(carry_s[...])

    i_end, low = lax.while_loop(lambda st: live(*st), body, (jnp.int32(0), jnp.float32(0.0)))

    @pl.when(low < SB_DEAD)
    def _():
        value_stage((qi - i_end) * tq, i_end & 1)

    out = acc_s[0]
    for h in range(1, nh):
        out = jnp.where((lane_q // SB_HEAD_DIM) == h, acc_s[h], out)
    o_ref[...] = out


def _stick_breaking(proj, batch, seq, width):
    tq = SB_KEY_TILE * SB_SUBTILES
    npair = width // LANES
    nq = seq // tq
    nh = LANES // SB_HEAD_DIM
    return pl.pallas_call(
        _sb_kernel,
        grid=(batch, npair, nq),
        in_specs=[pl.BlockSpec((tq, LANES), lambda b, p, i: (b * nq + i, p)),
                  pl.BlockSpec((seq, LANES), lambda b, p, i: (b, npair + p)),
                  pl.BlockSpec((seq, LANES), lambda b, p, i: (b, 2 * npair + p))],
        out_specs=pl.BlockSpec((tq, LANES), lambda b, p, i: (b * nq + i, p)),
        out_shape=jax.ShapeDtypeStruct((batch * seq, width), F32),
        scratch_shapes=[pltpu.VMEM((2, nh, tq, tq), F32),
                        pltpu.VMEM((2, nh * SB_SUBTILES, tq, 1), F32),
                        pltpu.VMEM((nh, tq, LANES), F32),
                        pltpu.VMEM((nh, tq, 1), F32)],
        compiler_params=pltpu.CompilerParams(
            dimension_semantics=("parallel", "parallel", "arbitrary"),
            vmem_limit_bytes=VMEM_LIMIT),
        name="stick_breaking",
    )(proj, proj, proj)


POST_MIX_PARTS = 2


def _post_mix_kernel(x_ref, mix_ref, mq_ref, mk_ref, mv_ref, wo_ref, g_ref, o_ref):
    dh = X_HEAD_DIM
    wmix = mix_ref.shape[1]
    mk = mk_ref[...].astype(MXU_DTYPE)
    mv = mv_ref[...].astype(MXU_DTYPE)
    half = x_ref.shape[0] // POST_MIX_PARTS
    for r in range(POST_MIX_PARTS):
        rows = slice(r * half, (r + 1) * half)
        mq = mq_ref[rows, :] * (dh ** -0.5)
        heads = []
        for h in range(X_HEADS):
            cols = slice(h * dh, (h + 1) * dh)
            s = _dot_nt(mq[:, cols], mk[:, cols])
            p = jnp.exp(s - jnp.max(s, axis=-1, keepdims=True))
            inv_l = 1.0 / jnp.sum(p, axis=-1, keepdims=True)
            heads.append(_dot(p, mv[:, cols]) * inv_l)
        cross = jnp.concatenate(heads, axis=-1)
        y = _dot(mix_ref[rows, :], wo_ref[:wmix, :]) + _dot(cross, wo_ref[wmix:, :])
        o_ref[rows, :] = x_ref[rows, :] + _rms(y, g_ref[...])


def _post_mix(x, mix, proj, memq_blk, mem_kv, layer, wo, g, batch, seq, tm):
    t, d = x.shape
    wmix = mix.shape[1]
    xw = X_HEADS * X_HEAD_DIM
    nt = seq // tm
    return pl.pallas_call(
        _post_mix_kernel,
        grid=(batch, nt),
        in_specs=[pl.BlockSpec((tm, d), lambda b, i: (b * nt + i, 0)),
                  pl.BlockSpec((tm, wmix), lambda b, i: (b * nt + i, 0)),
                  pl.BlockSpec((tm, xw), lambda b, i: (b * nt + i, memq_blk)),
                  pl.BlockSpec((N_MEM, xw), lambda b, i: (b, 2 * layer)),
                  pl.BlockSpec((N_MEM, xw), lambda b, i: (b, 2 * layer + 1)),
                  pl.BlockSpec((wmix + xw, d), lambda b, i: (0, 0)),
                  pl.BlockSpec((1, d), lambda b, i: (0, 0))],
        out_specs=pl.BlockSpec((tm, d), lambda b, i: (b * nt + i, 0)),
        out_shape=jax.ShapeDtypeStruct((t, d), F32),
        compiler_params=pltpu.CompilerParams(dimension_semantics=("parallel", "parallel"),
                                             vmem_limit_bytes=VMEM_LIMIT),
        name="post_mix",
    )(x, mix, proj, mem_kv, mem_kv, wo, g.reshape(1, d))


def _mlp_kernel(x_ref, g1_ref, wu_ref, wd_ref, g2_ref, o_ref, *, tf):
    x = x_ref[...]
    h = _rms(x, g1_ref[...]).astype(MXU_DTYPE)
    acc = jnp.zeros(x.shape, F32)
    for c in range(0, wu_ref.shape[1], tf):
        u = jnp.dot(h, wu_ref[:, c:c + tf], preferred_element_type=F32)
        u = jnp.square(jnp.maximum(u, 0.0)).astype(MXU_DTYPE)
        acc = acc + jnp.dot(u, wd_ref[c:c + tf, :], preferred_element_type=F32)
    o_ref[...] = x + _rms(acc, g2_ref[...])


def _mlp(x, g1, wu, wd, g2, tm, tf=1024):
    t, d = x.shape
    f = wu.shape[1]
    return pl.pallas_call(
        functools.partial(_mlp_kernel, tf=tf),
        grid=(t // tm,),
        in_specs=[pl.BlockSpec((tm, d), lambda i: (i, 0)),
                  pl.BlockSpec((1, d), lambda i: (0, 0)),
                  pl.BlockSpec((d, f), lambda i: (0, 0)),
                  pl.BlockSpec((f, d), lambda i: (0, 0)),
                  pl.BlockSpec((1, d), lambda i: (0, 0))],
        out_specs=pl.BlockSpec((tm, d), lambda i: (i, 0)),
        out_shape=jax.ShapeDtypeStruct((t, d), F32),
        compiler_params=pltpu.CompilerParams(dimension_semantics=("parallel",),
                                             vmem_limit_bytes=VMEM_LIMIT),
        name="mlp",
    )(x, g1.reshape(1, d), wu, wd, g2.reshape(1, d))


def kernel(x, mem, mem_norm, norm_pre_mix, norm_post_mix, norm_pre_mlp, norm_post_mlp,
           w_in_a, conv_w_a, a_log_a, dt_bias_a, onorm_a, w_in_b, w_mem_kv, w_out,
           w_up, w_down):
    batch, seq, d = x.shape
    depth = norm_pre_mix.shape[0]
    n_lin = a_log_a.shape[1]
    wseq = n_lin * GDN_HEAD_DIM
    xw = X_HEADS * X_HEAD_DIM
    tm = min(512, seq)
    wt = MXU_DTYPE

    xf = x.reshape(batch * seq, d)
    w_kv = jnp.concatenate([w_mem_kv[i] for i in range(depth)], axis=1).astype(wt)
    mem_kv = _norm_matmul(mem.reshape(batch * N_MEM, d), mem_norm, w_kv, tm=min(512, batch * N_MEM))

    for i in range(depth):
        j = i // 2
        if i % 2 == 0:
            w = w_in_a[j]
            n_main = 4 * wseq
            ba_pad = jnp.zeros((d, LANES - 2 * n_lin), w.dtype)
            w = jnp.concatenate([w[:, :n_main], w[:, n_main + 2 * n_lin:],
                                 w[:, n_main:n_main + 2 * n_lin], ba_pad], axis=1).astype(wt)
            proj = _norm_matmul(xf, norm_pre_mix[i], w, tm=tm)
            mix = _gdn(proj, conv_w_a[j], a_log_a[j], dt_bias_a[j], onorm_a[j], batch, seq, n_lin)
            memq_blk = n_main // xw
        else:
            proj = _norm_matmul(xf, norm_pre_mix[i], w_in_b[j].astype(wt), tm=tm)
            mix = _stick_breaking(proj, batch, seq, wseq)
            memq_blk = 3 * wseq // xw
        xf = _post_mix(xf, mix, proj, memq_blk, mem_kv, i, w_out[i].astype(wt), norm_post_mix[i],
                       batch, seq, tm)
        xf = _mlp(xf, norm_pre_mlp[i], w_up[i].astype(wt), w_down[i].astype(wt), norm_post_mlp[i], tm)
    return xf.reshape(batch, seq, d)
```
